```python
import jax, jax.numpy as jnp
from jax import lax
import numpy as np

D_MODEL = 4096
BATCH = 8
SEQ = 2048
DEPTH = 1
DEC_BATCH = 2
DEC_SEQ = 8192
PAST_LEN = 128

PLE_DIM = 256
D_FF = 11008
CHUNK = 128
GM_GROUPS = 16
GM_GROUP_DIM = 128
GM_WIDTH = GM_GROUPS * GM_GROUP_DIM
MLA_HEADS = 16
QK_NOPE_DIM = 128
QK_ROPE_DIM = 64
V_HEAD_DIM = 128
Q_LORA_RANK = 1024
KV_LORA_RANK = 512
ROPE_THETA = 10000.0
Q_BLOCK = 128
RMS_EPS = 1e-6
IN_WIDTH = 2 * GM_WIDTH + Q_LORA_RANK + KV_LORA_RANK + QK_ROPE_DIM + 2 * D_MODEL

kernel_name = "hybrid_gmlp_mla_macaron_encoder"


def rmsnorm(x, g):
    xf = x.astype(jnp.float32)
    y = xf * lax.rsqrt(jnp.mean(xf * xf, axis=-1, keepdims=True) + RMS_EPS)
    return (y * g.astype(jnp.float32)).astype(x.dtype)


def swiglu(h, wg, wu, wd):
    return (jax.nn.silu(h @ wg) * (h @ wu)) @ wd


def rope_tables(S, dtype):
    inv = 1.0 / (ROPE_THETA ** (jnp.arange(0, QK_ROPE_DIM, 2, dtype=jnp.float32) / QK_ROPE_DIM))
    ang = jnp.arange(S, dtype=jnp.float32)[:, None] * inv[None, :]
    return jnp.cos(ang).astype(dtype), jnp.sin(ang).astype(dtype)


def apply_rope(x, cos, sin):
    half = QK_ROPE_DIM // 2
    x1, x2 = x[..., :half], x[..., half:]
    return jnp.concatenate([x1 * cos - x2 * sin, x1 * sin + x2 * cos], axis=-1)


def gmlp_mixer(u, v, v_norm, ws, bs):
    B, S, _ = v.shape
    u = jax.nn.gelu(u)
    v = rmsnorm(jax.nn.gelu(v), v_norm)
    vc = v.reshape(B, S // CHUNK, CHUNK, GM_GROUPS, GM_GROUP_DIM)
    mixed = jnp.einsum('gpq,bnqgc->bnpgc', ws, vc) + bs.T[None, None, :, :, None]
    return u * mixed.reshape(B, S, GM_WIDTH)


def mla_mixer(q_lat, kv_lat, k_rope, q_norm, w_uq, kv_norm, w_ukv):
    B, S, _ = q_lat.shape
    H = MLA_HEADS
    q = (rmsnorm(q_lat, q_norm) @ w_uq).reshape(B, S, H, QK_NOPE_DIM + QK_ROPE_DIM)
    q_nope, q_rope = q[..., :QK_NOPE_DIM], q[..., QK_NOPE_DIM:]
    kv = (rmsnorm(kv_lat, kv_norm) @ w_ukv).reshape(B, S, H, QK_NOPE_DIM + V_HEAD_DIM)
    k_nope, v = kv[..., :QK_NOPE_DIM], kv[..., QK_NOPE_DIM:]
    cos, sin = rope_tables(S, q.dtype)
    q_rope = apply_rope(q_rope, cos[:, None, :], sin[:, None, :])
    k_rope = apply_rope(k_rope, cos, sin)
    scale = (QK_NOPE_DIM + QK_ROPE_DIM) ** -0.5
    nb = S // Q_BLOCK
    qn_b = q_nope.reshape(B, nb, Q_BLOCK, H, QK_NOPE_DIM).transpose(1, 0, 2, 3, 4)
    qr_b = q_rope.reshape(B, nb, Q_BLOCK, H, QK_ROPE_DIM).transpose(1, 0, 2, 3, 4)

    def attend(blk):
        qn, qr = blk
        s = jnp.einsum('bqhd,bkhd->bhqk', qn, k_nope) + jnp.einsum('bqhr,bkr->bhqk', qr, k_rope)
        pr = jax.nn.softmax(s.astype(jnp.float32) * scale, axis=-1).astype(v.dtype)
        return jnp.einsum('bhqk,bkhd->bqhd', pr, v)

    o = lax.map(attend, (qn_b, qr_b))
    return o.transpose(1, 0, 2, 3, 4).reshape(B, S, H * V_HEAD_DIM)


def encoder_layer(x, p, ffn1_norm, ffn1_wg, ffn1_wu, ffn1_wd, mix_norm, w_in, gm_v_norm, gm_ws, gm_bs,
                  w_out_a, q_norm, w_uq, kv_norm, w_ukv, w_out_b, w_out, ffn2_norm, ffn2_wg, ffn2_wu,
                  ffn2_wd, ple_gate_norm, w_ple_gate, w_ple_proj, ple_post_norm):
    x = x + 0.5 * swiglu(rmsnorm(x, ffn1_norm), ffn1_wg, ffn1_wu, ffn1_wd)
    z = rmsnorm(x, mix_norm) @ w_in
    widths = [GM_WIDTH, GM_WIDTH, Q_LORA_RANK, KV_LORA_RANK, QK_ROPE_DIM, D_MODEL]
    offs = [int(o) for o in np.cumsum(widths)]
    u, v, q_lat, kv_lat, k_rope, gate_a, gate_b = jnp.split(z, offs, axis=-1)
    y_a = gmlp_mixer(u, v, gm_v_norm, gm_ws, gm_bs) @ w_out_a
    y_b = mla_mixer(q_lat, kv_lat, k_rope, q_norm, w_uq, kv_norm, w_ukv) @ w_out_b
    merged = jax.nn.sigmoid(gate_a) * y_a + jax.nn.sigmoid(gate_b) * y_b
    x = x + merged @ w_out
    x = x + 0.5 * swiglu(rmsnorm(x, ffn2_norm), ffn2_wg, ffn2_wu, ffn2_wd)
    g = jax.nn.sigmoid(rmsnorm(x, ple_gate_norm) @ w_ple_gate)
    x = x + rmsnorm(g * (p @ w_ple_proj), ple_post_norm)
    return x


def setup_inputs(seed: int = 0) -> dict:
    key = jax.random.key(seed)
    ks = iter(jax.random.split(key, 40))
    L, D = DEPTH, D_MODEL

    def dense(shape, fan_in):
        return jax.random.normal(next(ks), shape, jnp.float32) * (fan_in ** -0.5)

    def gain(shape):
        return 1.0 + 0.01 * jax.random.normal(next(ks), shape, jnp.float32)

    return {
        "x_prompt": jax.random.normal(next(ks), (BATCH, SEQ, D), jnp.float32),
        "x_sample": jax.random.normal(next(ks), (DEC_BATCH, DEC_SEQ, D), jnp.float32),
        "p_prompt": jax.random.normal(next(ks), (DEPTH, BATCH, SEQ, PLE_DIM), jnp.float32),
        "p_sample": jax.random.normal(next(ks), (DEPTH, DEC_BATCH, DEC_SEQ, PLE_DIM), jnp.float32),
        "ffn1_norm": gain((L, D)),
        "ffn1_wg": dense((L, D, D_FF), D),
        "ffn1_wu": dense((L, D, D_FF), D),
        "ffn1_wd": dense((L, D_FF, D), D_FF),
        "mix_norm": gain((L, D)),
        "w_in": dense((L, D, IN_WIDTH), D),
        "gm_v_norm": gain((L, GM_WIDTH)),
        "gm_ws": dense((L, GM_GROUPS, CHUNK, CHUNK), CHUNK),
        "gm_bs": 1.0 + 0.1 * jax.random.normal(next(ks), (L, GM_GROUPS, CHUNK), jnp.float32),
        "w_out_a": dense((L, GM_WIDTH, D), GM_WIDTH),
        "q_norm": gain((L, Q_LORA_RANK)),
        "w_uq": dense((L, Q_LORA_RANK, MLA_HEADS * (QK_NOPE_DIM + QK_ROPE_DIM)), Q_LORA_RANK),
        "kv_norm": gain((L, KV_LORA_RANK)),
        "w_ukv": dense((L, KV_LORA_RANK, MLA_HEADS * (QK_NOPE_DIM + V_HEAD_DIM)), KV_LORA_RANK),
        "w_out_b": dense((L, MLA_HEADS * V_HEAD_DIM, D), MLA_HEADS * V_HEAD_DIM),
        "w_out": dense((L, D, D), D),
        "ffn2_norm": gain((L, D)),
        "ffn2_wg": dense((L, D, D_FF), D),
        "ffn2_wu": dense((L, D, D_FF), D),
        "ffn2_wd": dense((L, D_FF, D), D_FF),
        "ple_gate_norm": gain((L, D)),
        "w_ple_gate": dense((L, D, D), D),
        "w_ple_proj": dense((L, PLE_DIM, D), PLE_DIM),
        "ple_post_norm": gain((L, D)),
        "final_norm": gain((D,)),
    }


def reference(x_prompt, x_sample, p_prompt, p_sample, ffn1_norm, ffn1_wg, ffn1_wu, ffn1_wd, mix_norm, w_in,
              gm_v_norm, gm_ws, gm_bs, w_out_a, q_norm, w_uq, kv_norm, w_ukv, w_out_b, w_out, ffn2_norm,
              ffn2_wg, ffn2_wu, ffn2_wd, ple_gate_norm, w_ple_gate, w_ple_proj, ple_post_norm, final_norm):
    layer_w = (ffn1_norm, ffn1_wg, ffn1_wu, ffn1_wd, mix_norm, w_in, gm_v_norm, gm_ws, gm_bs, w_out_a,
               q_norm, w_uq, kv_norm, w_ukv, w_out_b, w_out, ffn2_norm, ffn2_wg, ffn2_wu, ffn2_wd,
               ple_gate_norm, w_ple_gate, w_ple_proj, ple_post_norm)

    def trunk(x, p):
        for l in range(DEPTH):
            x = encoder_layer(x, p[l], *[w[l] for w in layer_w])
        return rmsnorm(x, final_norm)

    y_prompt = trunk(x_prompt, p_prompt)
    y_sample = trunk(x_sample, p_sample)
    return (y_prompt, y_sample)
```

```python
import functools

import jax
import jax.numpy as jnp
import numpy as np
from jax import lax
from jax.experimental import pallas as pl
from jax.experimental.pallas import tpu as pltpu

F32 = jnp.float32
BF16 = jnp.bfloat16

RMS_EPS = 1e-6
ROPE_THETA = 10000.0
QK_NOPE_DIM = 128
QK_ROPE_DIM = 64
V_HEAD_DIM = 128
LANES = 128
HEAD_PAD = 2 * LANES
FF_ALIGN = 1024
VMEM_LIMIT = 56 * 1024 * 1024


def _cparams(*sem):
    return pltpu.CompilerParams(dimension_semantics=sem, vmem_limit_bytes=VMEM_LIMIT)


def _tile(n, want):
    if n <= want:
        return n
    t = want
    while n % t:
        t //= 2
    return t


def _sigmoid(x):
    return 1.0 / (1.0 + jnp.exp(-x))


def _gelu_tanh(x):
    c = np.float32(np.sqrt(2.0 / np.pi))
    return x * (0.5 * (1.0 + jnp.tanh(c * (x + 0.044715 * (x * x * x)))))


def _rms_scale(x):
    return lax.rsqrt(jnp.mean(x * x, axis=-1, keepdims=True) + RMS_EPS)


def _rmsnorm_cast_kernel(x_ref, g_ref, o_ref):
    x = x_ref[...]
    o_ref[...] = ((x * _rms_scale(x)) * g_ref[...]).astype(o_ref.dtype)


def rmsnorm_cast(x, g):
    m, d = x.shape
    tm = _tile(m, 256)
    return pl.pallas_call(
        _rmsnorm_cast_kernel,
        grid=(m // tm,),
        in_specs=[pl.BlockSpec((tm, d), lambda i: (i, 0)), pl.BlockSpec((1, d), lambda i: (0, 0))],
        out_specs=pl.BlockSpec((tm, d), lambda i: (i, 0)),
        out_shape=jax.ShapeDtypeStruct((m, d), BF16),
        compiler_params=_cparams("parallel"),
        name="rmsnorm_cast",
    )(x, g.reshape(1, d))


def _mm_kernel(a_ref, w_ref, o_ref):
    o_ref[...] = jnp.dot(a_ref[...], w_ref[...], preferred_element_type=F32).astype(o_ref.dtype)


def matmul(a, w, out_dtype, tm, tn, name):
    m, k = a.shape
    n = w.shape[1]
    tm, tn = _tile(m, tm), _tile(n, tn)
    return pl.pallas_call(
        _mm_kernel,
        grid=(m // tm, n // tn),
        in_specs=[pl.BlockSpec((tm, k), lambda i, j: (i, 0)), pl.BlockSpec((k, tn), lambda i, j: (0, j))],
        out_specs=pl.BlockSpec((tm, tn), lambda i, j: (i, j)),
        out_shape=jax.ShapeDtypeStruct((m, n), out_dtype),
        compiler_params=_cparams("parallel", "parallel"),
        name=name,
    )(a, w)


def _mm_residual_kernel(a_ref, w_ref, x_ref, o_ref, *, scale):
    acc = jnp.dot(a_ref[...], w_ref[...], preferred_element_type=F32)
    o_ref[...] = x_ref[...] + (acc if scale == 1.0 else scale * acc)


def matmul_residual(a, w, x, scale, tm, tn, name):
    m, k = a.shape
    n = w.shape[1]
    tm, tn = _tile(m, tm), _tile(n, tn)
    return pl.pallas_call(
        functools.partial(_mm_residual_kernel, scale=scale),
        grid=(m // tm, n // tn),
        in_specs=[
            pl.BlockSpec((tm, k), lambda i, j: (i, 0)),
            pl.BlockSpec((k, tn), lambda i, j: (0, j)),
            pl.BlockSpec((tm, tn), lambda i, j: (i, j)),
        ],
        out_specs=pl.BlockSpec((tm, tn), lambda i, j: (i, j)),
        out_shape=jax.ShapeDtypeStruct((m, n), F32),
        compiler_params=_cparams("parallel", "parallel"),
        name=name,
    )(a, w, x)


def _ffn_up_kernel(h_ref, wg_ref, wu_ref, o_ref):
    h = h_ref[...]
    g = jnp.dot(h, wg_ref[...], preferred_element_type=F32)
    u = jnp.dot(h, wu_ref[...], preferred_element_type=F32)
    o_ref[...] = ((g * _sigmoid(g)) * u).astype(o_ref.dtype)


def ffn_up(h, wg, wu):
    m, k = h.shape
    n = wg.shape[1]
    tm, tn = _tile(m, 1024), _tile(n, 512)
    return pl.pallas_call(
        _ffn_up_kernel,
        grid=(m // tm, n // tn),
        in_specs=[
            pl.BlockSpec((tm, k), lambda i, j: (i, 0)),
            pl.BlockSpec((k, tn), lambda i, j: (0, j)),
            pl.BlockSpec((k, tn), lambda i, j: (0, j)),
        ],
        out_specs=pl.BlockSpec((tm, tn), lambda i, j: (i, j)),
        out_shape=jax.ShapeDtypeStruct((m, n), BF16),
        compiler_params=_cparams("parallel", "parallel"),
        name="ffn_up",
    )(h, wg, wu)


def ffn_half_step(x, norm_g, wg, wu, wd):
    h = rmsnorm_cast(x, norm_g)
    a = ffn_up(h, wg, wu)
    return matmul_residual(a, wd, x, 0.5, 512, 256, "ffn_down")


def _gmlp_kernel(u_ref, v_ref, vg_ref, ws_ref, bs_ref, o_ref, vn_ref, *, chunk, groups, gdim):
    v = _gelu_tanh(v_ref[...])
    vn_ref[...] = ((v * _rms_scale(v)) * vg_ref[...]).astype(vn_ref.dtype)
    tm = u_ref.shape[0]
    for c in range(tm // chunk):
        rows = slice(c * chunk, (c + 1) * chunk)
        for g in range(groups):
            cols = slice(g * gdim, (g + 1) * gdim)
            mixed = jnp.dot(ws_ref[g], vn_ref[rows, cols], preferred_element_type=F32) + bs_ref[g]
            o_ref[rows, cols] = (_gelu_tanh(u_ref[rows, cols]) * mixed).astype(o_ref.dtype)


def gmlp_gate(uv, v_norm, ws, bs_b):
    m = uv.shape[0]
    groups, chunk, _ = ws.shape
    gdim = bs_b.shape[2]
    w = groups * gdim
    tm = _tile(m, 2 * chunk)
    return pl.pallas_call(
        functools.partial(_gmlp_kernel, chunk=chunk, groups=groups, gdim=gdim),
        grid=(m // tm,),
        in_specs=[
            pl.BlockSpec((tm, w), lambda i: (i, 0)),
            pl.BlockSpec((tm, w), lambda i: (i, 1)),
            pl.BlockSpec((1, w), lambda i: (0, 0)),
            pl.BlockSpec((groups, chunk, chunk), lambda i: (0, 0, 0)),
            pl.BlockSpec((groups, chunk, gdim), lambda i: (0, 0, 0)),
        ],
        out_specs=pl.BlockSpec((tm, w), lambda i: (i, 0)),
        out_shape=jax.ShapeDtypeStruct((m, w), BF16),
        scratch_shapes=[pltpu.VMEM((tm, w), BF16)],
        compiler_params=_cparams("parallel"),
        name="gmlp_gate",
    )(uv, uv, v_norm.reshape(1, w), ws, bs_b)


def _rope_block(blk, c1, c2):
    return blk * c1 + pltpu.roll(blk, LANES // 2, 1) * c2


def _q_proj_kernel(ql_ref, g_ref, w_ref, c1_ref, c2_ref, o_ref, hn_ref, *, heads_per_tile):
    @pl.when(pl.program_id(1) == 0)
    def _():
        x = ql_ref[...]
        hn_ref[...] = ((x * _rms_scale(x)) * g_ref[...]).astype(hn_ref.dtype)

    acc = jnp.dot(hn_ref[...], w_ref[...], preferred_element_type=F32)
    c1, c2 = c1_ref[...], c2_ref[...]
    for h in range(heads_per_tile):
        lo = h * HEAD_PAD
        o_ref[:, lo:lo + LANES] = acc[:, lo:lo + LANES].astype(o_ref.dtype)
        o_ref[:, lo + LANES:lo + HEAD_PAD] = _rope_block(acc[:, lo + LANES:lo + HEAD_PAD], c1, c2).astype(o_ref.dtype)


def q_proj(lat, q_norm, w_uq_p, c1, c2, q_rank, seq):
    m = lat.shape[0]
    n = w_uq_p.shape[1]
    tm = _tile(seq, 512)
    tn = _tile(n, 4 * HEAD_PAD)
    pos_blocks = seq // tm
    return pl.pallas_call(
        functools.partial(_q_proj_kernel, heads_per_tile=tn // HEAD_PAD),
        grid=(m // tm, n // tn),
        in_specs=[
            pl.BlockSpec((tm, q_rank), lambda i, j: (i, 0)),
            pl.BlockSpec((1, q_rank), lambda i, j: (0, 0)),
            pl.BlockSpec((q_rank, tn), lambda i, j: (0, j)),
            pl.BlockSpec((tm, LANES), lambda i, j: (i % pos_blocks, 0)),
            pl.BlockSpec((tm, LANES), lambda i, j: (i % pos_blocks, 0)),
        ],
        out_specs=pl.BlockSpec((tm, tn), lambda i, j: (i, j)),
        out_shape=jax.ShapeDtypeStruct((m, n), BF16),
        scratch_shapes=[pltpu.VMEM((tm, q_rank), BF16)],
        compiler_params=_cparams("parallel", "arbitrary"),
        name="q_proj",
    )(lat, q_norm.reshape(1, q_rank), w_uq_p, c1, c2)


def _kv_proj_kernel(kvl_ref, kr_ref, g_ref, wk_ref, wv_ref, c1_ref, c2_ref, k_ref, v_ref, *, heads):
    x = kvl_ref[...]
    hn = ((x * _rms_scale(x)) * g_ref[...]).astype(BF16)
    kn = jnp.dot(hn, wk_ref[...], preferred_element_type=F32)
    v_ref[...] = jnp.dot(hn, wv_ref[...], preferred_element_type=F32).astype(v_ref.dtype)
    kr = _rope_block(kr_ref[...], c1_ref[...], c2_ref[...]).astype(k_ref.dtype)
    for h in range(heads):
        lo = h * HEAD_PAD
        k_ref[:, lo:lo + LANES] = kn[:, h * QK_NOPE_DIM:(h + 1) * QK_NOPE_DIM].astype(k_ref.dtype)
        k_ref[:, lo + LANES:lo + HEAD_PAD] = kr


def kv_proj(lat, kv_norm, w_k, w_v, c1, c2, q_rank, kv_rank, heads, seq):
    m = lat.shape[0]
    tm = _tile(seq, 512)
    pos_blocks = seq // tm
    kv_blk = q_rank // kv_rank
    kr_blk = (q_rank + kv_rank) // LANES
    return pl.pallas_call(
        functools.partial(_kv_proj_kernel, heads=heads),
        grid=(m // tm,),
        in_specs=[
            pl.BlockSpec((tm, kv_rank), lambda i: (i, kv_blk)),
            pl.BlockSpec((tm, LANES), lambda i: (i, kr_blk)),
            pl.BlockSpec((1, kv_rank), lambda i: (0, 0)),
            pl.BlockSpec(w_k.shape, lambda i: (0, 0)),
            pl.BlockSpec(w_v.shape, lambda i: (0, 0)),
            pl.BlockSpec((tm, LANES), lambda i: (i % pos_blocks, 0)),
            pl.BlockSpec((tm, LANES), lambda i: (i % pos_blocks, 0)),
        ],
        out_specs=[
            pl.BlockSpec((tm, heads * HEAD_PAD), lambda i: (i, 0)),
            pl.BlockSpec((tm, heads * V_HEAD_DIM), lambda i: (i, 0)),
        ],
        out_shape=[
            jax.ShapeDtypeStruct((m, heads * HEAD_PAD), BF16),
            jax.ShapeDtypeStruct((m, heads * V_HEAD_DIM), BF16),
        ],
        compiler_params=_cparams("parallel"),
        name="kv_proj",
    )(lat, lat, kv_norm.reshape(1, kv_rank), w_k, w_v, c1, c2)


def _attn_kernel(q_ref, k_ref, v_ref, o_ref, *, tq, tk, scale):
    seq = k_ref.shape[0]

    def q_body(qi, _):
        q0 = pl.multiple_of(qi * tq, tq)
        q = q_ref[pl.ds(q0, tq), :]

        def kv_body(ki, carry):
            m, l, acc = carry
            k0 = pl.multiple_of(ki * tk, tk)
            k = k_ref[pl.ds(k0, tk), :]
            s = lax.dot_general(q, k, (((1,), (1,)), ((), ())), preferred_element_type=F32) * scale
            m_new = jnp.maximum(m, jnp.max(s, axis=-1, keepdims=True))
            p = jnp.exp(s - m_new)
            alpha = jnp.exp(m - m_new)
            l = alpha * l + jnp.sum(p, axis=-1, keepdims=True)
            pv = jnp.dot(p.astype(BF16), v_ref[pl.ds(k0, tk), :], preferred_element_type=F32)
            return m_new, l, alpha * acc + pv

        init = (jnp.full((tq, 1), -jnp.inf, F32), jnp.zeros((tq, 1), F32), jnp.zeros((tq, V_HEAD_DIM), F32))
        _, l, acc = lax.fori_loop(0, seq // tk, kv_body, init)
        o_ref[pl.ds(q0, tq), :] = (acc / l).astype(o_ref.dtype)
        return 0

    lax.fori_loop(0, seq // tq, q_body, 0)


def attention(q, k, v, batch, seq, heads):
    m = q.shape[0]
    tq, tk = _tile(seq, 256), _tile(seq, 512)
    scale = float((QK_NOPE_DIM + QK_ROPE_DIM) ** -0.5)
    return pl.pallas_call(
        functools.partial(_attn_kernel, tq=tq, tk=tk, scale=scale),
        grid=(batch, heads),
        in_specs=[
            pl.BlockSpec((seq, HEAD_PAD), lambda b, h: (b, h)),
            pl.BlockSpec((seq, HEAD_PAD), lambda b, h: (b, h)),
            pl.BlockSpec((seq, V_HEAD_DIM), lambda b, h: (b, h)),
        ],
        out_specs=pl.BlockSpec((seq, V_HEAD_DIM), lambda b, h: (b, h)),
        out_shape=jax.ShapeDtypeStruct((m, heads * V_HEAD_DIM), BF16),
        compiler_params=_cparams("parallel", "parallel"),
        name="attention",
    )(q, k, v)


def _merge_kernel(a_ref, b_ref, wa_ref, wb_ref, ga_ref, gb_ref, o_ref):
    ya = jnp.dot(a_ref[...], wa_ref[...], preferred_element_type=F32)
    yb = jnp.dot(b_ref[...], wb_ref[...], preferred_element_type=F32)
    o_ref[...] = (_sigmoid(ga_ref[...]) * ya + _sigmoid(gb_ref[...]) * yb).astype(o_ref.dtype)


def gated_merge(a, b, wa, wb, gates):
    m, ka = a.shape
    kb = b.shape[1]
    n = wa.shape[1]
    tm, tn = _tile(m, 1024), _tile(n, 512)
    nj = n // tn
    return pl.pallas_call(
        _merge_kernel,
        grid=(m // tm, nj),
        in_specs=[
            pl.BlockSpec((tm, ka), lambda i, j: (i, 0)),
            pl.BlockSpec((tm, kb), lambda i, j: (i, 0)),
            pl.BlockSpec((ka, tn), lambda i, j: (0, j)),
            pl.BlockSpec((kb, tn), lambda i, j: (0, j)),
            pl.BlockSpec((tm, tn), lambda i, j: (i, j)),
            pl.BlockSpec((tm, tn), lambda i, j: (i, j + nj)),
        ],
        out_specs=pl.BlockSpec((tm, tn), lambda i, j: (i, j)),
        out_shape=jax.ShapeDtypeStruct((m, n), BF16),
        compiler_params=_cparams("parallel", "parallel"),
        name="gated_merge",
    )(a, b, wa, wb, gates, gates)


def _ple_kernel(h_ref, wg_ref, p_ref, wp_ref, o_ref):
    g = jnp.dot(h_ref[...], wg_ref[...], preferred_element_type=F32)
    pp = jnp.dot(p_ref[...].astype(BF16), wp_ref[...], preferred_element_type=F32)
    o_ref[...] = _sigmoid(g) * pp


def ple_gate(h, wg, p, wp):
    m, k = h.shape
    kp = p.shape[1]
    n = wg.shape[1]
    tm, tn = _tile(m, 1024), _tile(n, 512)
    return pl.pallas_call(
        _ple_kernel,
        grid=(m // tm, n // tn),
        in_specs=[
            pl.BlockSpec((tm, k), lambda i, j: (i, 0)),
            pl.BlockSpec((k, tn), lambda i, j: (0, j)),
            pl.BlockSpec((tm, kp), lambda i, j: (i, 0)),
            pl.BlockSpec((kp, tn), lambda i, j: (0, j)),
        ],
        out_specs=pl.BlockSpec((tm, tn), lambda i, j: (i, j)),
        out_shape=jax.ShapeDtypeStruct((m, n), F32),
        compiler_params=_cparams("parallel", "parallel"),
        name="ple_gate",
    )(h, wg, p, wp)


def _final_kernel(x_ref, t_ref, gp_ref, gf_ref, o_ref):
    t = t_ref[...]
    x = x_ref[...] + (t * _rms_scale(t)) * gp_ref[...]
    o_ref[...] = (x * _rms_scale(x)) * gf_ref[...]


def ple_add_final_norm(x, t, post_g, final_g):
    m, d = x.shape
    tm = _tile(m, 256)
    row = pl.BlockSpec((tm, d), lambda i: (i, 0))
    vec = pl.BlockSpec((1, d), lambda i: (0, 0))
    return pl.pallas_call(
        _final_kernel,
        grid=(m // tm,),
        in_specs=[row, row, vec, vec],
        out_specs=row,
        out_shape=jax.ShapeDtypeStruct((m, d), F32),
        compiler_params=_cparams("parallel"),
        name="ple_add_final_norm",
    )(x, t, post_g.reshape(1, d), final_g.reshape(1, d))


def _rope_tables(seq):
    inv = 1.0 / (ROPE_THETA ** (jnp.arange(0, QK_ROPE_DIM, 2, dtype=F32) / QK_ROPE_DIM))
    ang = jnp.arange(seq, dtype=F32)[:, None] * inv[None, :]
    cos, sin = jnp.cos(ang), jnp.sin(ang)
    zero = jnp.zeros((seq, LANES - QK_ROPE_DIM), F32)
    return jnp.concatenate([cos, cos, zero], axis=1), jnp.concatenate([-sin, sin, zero], axis=1)


def _rotate_half_cols(w):
    half = QK_ROPE_DIM // 2
    return jnp.concatenate([w[..., half:], w[..., :half]], axis=-1)


def _prepare_weights(ffn1_wg, ffn1_wu, ffn1_wd, w_in, gm_ws, gm_bs, w_out_a, w_uq, w_ukv, w_out_b, w_out,
                     ffn2_wg, ffn2_wu, ffn2_wd, w_ple_gate, w_ple_proj, q_rank, kv_rank, gm_width):
    d_ff = ffn1_wg.shape[1]
    ff_pad = -d_ff % FF_ALIGN

    def up(w):
        return jnp.pad(w.astype(BF16), ((0, 0), (0, ff_pad)))

    def down(w):
        return jnp.pad(w.astype(BF16), ((0, ff_pad), (0, 0)))

    o_q = 2 * gm_width
    o_kv = o_q + q_rank
    o_kr = o_kv + kv_rank
    o_gate = o_kr + QK_ROPE_DIM
    w_kr = w_in[:, o_kr:o_gate]
    w_lat = jnp.concatenate([w_in[:, o_q:o_kr], w_kr, _rotate_half_cols(w_kr)], axis=1)

    heads = w_uq.shape[1] // (QK_NOPE_DIM + QK_ROPE_DIM)
    uq = w_uq.reshape(q_rank, heads, QK_NOPE_DIM + QK_ROPE_DIM)
    uq_rope = uq[:, :, QK_NOPE_DIM:]
    w_uq_p = jnp.concatenate([uq[:, :, :QK_NOPE_DIM], uq_rope, _rotate_half_cols(uq_rope)], axis=2)
    ukv = w_ukv.reshape(kv_rank, heads, QK_NOPE_DIM + V_HEAD_DIM)

    gdim = gm_width // gm_ws.shape[0]
    return dict(
        ffn1=(up(ffn1_wg), up(ffn1_wu), down(ffn1_wd)),
        ffn2=(up(ffn2_wg), up(ffn2_wu), down(ffn2_wd)),
        w_uv=w_in[:, :o_q].astype(BF16),
        w_lat=w_lat.astype(BF16),
        w_gates=w_in[:, o_gate:].astype(BF16),
        ws=gm_ws.astype(BF16),
        bs_b=jnp.broadcast_to(gm_bs[:, :, None], gm_bs.shape + (gdim,)).astype(F32),
        w_out_a=w_out_a.astype(BF16),
        w_uq_p=w_uq_p.reshape(q_rank, heads * HEAD_PAD).astype(BF16),
        w_k=ukv[:, :, :QK_NOPE_DIM].reshape(kv_rank, heads * QK_NOPE_DIM).astype(BF16),
        w_v=ukv[:, :, QK_NOPE_DIM:].reshape(kv_rank, heads * V_HEAD_DIM).astype(BF16),
        w_out_b=w_out_b.astype(BF16),
        w_out=w_out.astype(BF16),
        w_ple_gate=w_ple_gate.astype(BF16),
        w_ple_proj=w_ple_proj.astype(BF16),
        heads=heads,
    )


def _trunk(x3d, p3d, w, norms, q_rank, kv_rank):
    batch, seq, d = x3d.shape
    x = x3d.reshape(batch * seq, d)
    p = p3d.reshape(batch * seq, p3d.shape[-1])
    heads = w["heads"]

    x = ffn_half_step(x, norms["ffn1"], *w["ffn1"])

    n = rmsnorm_cast(x, norms["mix"])
    uv = matmul(n, w["w_uv"], F32, 1024, 512, "in_proj_uv")
    lat = matmul(n, w["w_lat"], F32, 512, w["w_lat"].shape[1], "in_proj_lat")
    gates = matmul(n, w["w_gates"], F32, 1024, 512, "in_proj_gates")

    gm = gmlp_gate(uv, norms["gm_v"], w["ws"], w["bs_b"])

    c1, c2 = _rope_tables(seq)
    q = q_proj(lat, norms["q"], w["w_uq_p"], c1, c2, q_rank, seq)
    k, v = kv_proj(lat, norms["kv"], w["w_k"], w["w_v"], c1, c2, q_rank, kv_rank, heads, seq)
    o = attention(q, k, v, batch, seq, heads)

    merged = gated_merge(gm, o, w["w_out_a"], w["w_out_b"], gates)
    x = matmul_residual(merged, w["w_out"], x, 1.0, 1024, 512, "out_proj")

    x = ffn_half_step(x, norms["ffn2"], *w["ffn2"])

    h = rmsnorm_cast(x, norms["ple_gate"])
    t = ple_gate(h, w["w_ple_gate"], p, w["w_ple_proj"])
    y = ple_add_final_norm(x, t, norms["ple_post"], norms["final"])
    return y.reshape(batch, seq, d)


def kernel(x_prompt, x_sample, p_prompt, p_sample, ffn1_norm, ffn1_wg, ffn1_wu, ffn1_wd, mix_norm, w_in, gm_v_norm, gm_ws, gm_bs, w_out_a, q_norm, w_uq, kv_norm, w_ukv, w_out_b, w_out, ffn2_norm, ffn2_wg, ffn2_wu, ffn2_wd, ple_gate_norm, w_ple_gate, w_ple_proj, ple_post_norm, final_norm):
    depth = ffn1_norm.shape[0]
    q_rank = q_norm.shape[1]
    kv_rank = kv_norm.shape[1]
    gm_width = gm_v_norm.shape[1]
    assert q_rank % kv_rank == 0 and (q_rank + kv_rank) % LANES == 0

    xs = [x_prompt, x_sample]
    for l in range(depth):
        w = _prepare_weights(ffn1_wg[l], ffn1_wu[l], ffn1_wd[l], w_in[l], gm_ws[l], gm_bs[l], w_out_a[l], w_uq[l],
                             w_ukv[l], w_out_b[l], w_out[l], ffn2_wg[l], ffn2_wu[l], ffn2_wd[l], w_ple_gate[l],
                             w_ple_proj[l], q_rank, kv_rank, gm_width)
        norms = dict(ffn1=ffn1_norm[l], mix=mix_norm[l], gm_v=gm_v_norm[l], q=q_norm[l], kv=kv_norm[l],
                     ffn2=ffn2_norm[l], ple_gate=ple_gate_norm[l], ple_post=ple_post_norm[l], final=final_norm)
        assert depth == 1
        xs = [_trunk(x, p[l], w, norms, q_rank, kv_rank) for x, p in zip(xs, [p_prompt, p_sample])]
    return tuple(xs)
```

```python
import functools

import jax
import jax.numpy as jnp
import numpy as np
from jax import lax
from jax.experimental import pallas as pl
from jax.experimental.pallas import tpu as pltpu

F32 = jnp.float32
BF16 = jnp.bfloat16

RMS_EPS = 1e-6
ROPE_THETA = 10000.0
QK_NOPE_DIM = 128
QK_ROPE_DIM = 64
V_HEAD_DIM = 128
LANES = 128
HEAD_PAD = 2 * LANES
QK_LOG2_SCALE = float((QK_NOPE_DIM + QK_ROPE_DIM) ** -0.5 * np.log2(np.e))
VMEM_LIMIT = 56 * 1024 * 1024


def _cparams(*sem):
    return pltpu.CompilerParams(dimension_semantics=sem, vmem_limit_bytes=VMEM_LIMIT)


def _tile(n, want):
    if n <= want:
        return n
    t = want
    while n % t:
        t //= 2
    return t


def _sigmoid(x):
    return 1.0 / (1.0 + jnp.exp(-x))


def _gelu_tanh(x):
    c = np.float32(np.sqrt(2.0 / np.pi))
    return x * (0.5 * (1.0 + jnp.tanh(c * (x + 0.044715 * (x * x * x)))))


def _rms_scale(x):
    return lax.rsqrt(jnp.mean(x * x, axis=-1, keepdims=True) + RMS_EPS)


def _rmsnorm_cast_kernel(x_ref, g_ref, o_ref):
    x = x_ref[...]
    o_ref[...] = ((x * _rms_scale(x)) * g_ref[...]).astype(o_ref.dtype)


def rmsnorm_cast(x, g):
    m, d = x.shape
    tm = _tile(m, 256)
    return pl.pallas_call(
        _rmsnorm_cast_kernel,
        grid=(m // tm,),
        in_specs=[pl.BlockSpec((tm, d), lambda i: (i, 0)), pl.BlockSpec((1, d), lambda i: (0, 0))],
        out_specs=pl.BlockSpec((tm, d), lambda i: (i, 0)),
        out_shape=jax.ShapeDtypeStruct((m, d), BF16),
        compiler_params=_cparams("parallel"),
        name="rmsnorm_cast",
    )(x, g.reshape(1, d))


def _mm_kernel(a_ref, w_ref, o_ref):
    o_ref[...] = jnp.dot(a_ref[...], w_ref[...], preferred_element_type=F32).astype(o_ref.dtype)


def matmul(a, w, out_dtype, tm, tn, name):
    m, k = a.shape
    n = w.shape[1]
    tm, tn = _tile(m, tm), _tile(n, tn)
    return pl.pallas_call(
        _mm_kernel,
        grid=(m // tm, n // tn),
        in_specs=[pl.BlockSpec((tm, k), lambda i, j: (i, 0)), pl.BlockSpec((k, tn), lambda i, j: (0, j))],
        out_specs=pl.BlockSpec((tm, tn), lambda i, j: (i, j)),
        out_shape=jax.ShapeDtypeStruct((m, n), out_dtype),
        compiler_params=_cparams("parallel", "parallel"),
        name=name,
    )(a, w)


def _mm_residual_kernel(a_ref, w_ref, x_ref, o_ref, *, scale):
    acc = jnp.dot(a_ref[...], w_ref[...], preferred_element_type=F32)
    o_ref[...] = x_ref[...] + (acc if scale == 1.0 else scale * acc)


def matmul_residual(a, w, x, scale, tm, tn, name):
    m, k = a.shape
    n = w.shape[1]
    tm, tn = _tile(m, tm), _tile(n, tn)
    return pl.pallas_call(
        functools.partial(_mm_residual_kernel, scale=scale),
        grid=(m // tm, n // tn),
        in_specs=[
            pl.BlockSpec((tm, k), lambda i, j: (i, 0)),
            pl.BlockSpec((k, tn), lambda i, j: (0, j)),
            pl.BlockSpec((tm, tn), lambda i, j: (i, j)),
        ],
        out_specs=pl.BlockSpec((tm, tn), lambda i, j: (i, j)),
        out_shape=jax.ShapeDtypeStruct((m, n), F32),
        compiler_params=_cparams("parallel", "parallel"),
        name=name,
    )(a, w, x)


def _ffn_up_kernel(h_ref, wg_ref, wu_ref, o_ref):
    h = h_ref[...]
    g = jnp.dot(h, wg_ref[...], preferred_element_type=F32)
    u = jnp.dot(h, wu_ref[...], preferred_element_type=F32)
    o_ref[...] = ((g * _sigmoid(g)) * u).astype(o_ref.dtype)


def ffn_up(h, wg, wu):
    m, k = h.shape
    n = wg.shape[1]
    tm, tn = _tile(m, 2048), _tile(n, 256)
    return pl.pallas_call(
        _ffn_up_kernel,
        grid=(m // tm, n // tn),
        in_specs=[
            pl.BlockSpec((tm, k), lambda i, j: (i, 0)),
            pl.BlockSpec((k, tn), lambda i, j: (0, j)),
            pl.BlockSpec((k, tn), lambda i, j: (0, j)),
        ],
        out_specs=pl.BlockSpec((tm, tn), lambda i, j: (i, j)),
        out_shape=jax.ShapeDtypeStruct((m, n), BF16),
        compiler_params=_cparams("parallel", "parallel"),
        name="ffn_up",
    )(h, wg, wu)


def ffn_half_step(x, norm_g, wg, wu, wd):
    h = rmsnorm_cast(x, norm_g)
    a = ffn_up(h, wg, wu)
    return matmul_residual(a, wd, x, 0.5, 512, 512, "ffn_down")


def _gmlp_kernel(u_ref, v_ref, vg_ref, ws_ref, bs_ref, o_ref, vn_ref, *, chunk, groups, gdim):
    v = _gelu_tanh(v_ref[...])
    vn_ref[...] = ((v * _rms_scale(v)) * vg_ref[...]).astype(vn_ref.dtype)
    tm = u_ref.shape[0]
    for c in range(tm // chunk):
        rows = slice(c * chunk, (c + 1) * chunk)
        for g in range(groups):
            cols = slice(g * gdim, (g + 1) * gdim)
            mixed = jnp.dot(ws_ref[g], vn_ref[rows, cols], preferred_element_type=F32) + bs_ref[g]
            o_ref[rows, cols] = (_gelu_tanh(u_ref[rows, cols]) * mixed).astype(o_ref.dtype)


def gmlp_gate(uv, v_norm, ws, bs_b):
    m = uv.shape[0]
    groups, chunk, _ = ws.shape
    gdim = bs_b.shape[2]
    w = groups * gdim
    tm = _tile(m, 2 * chunk)
    return pl.pallas_call(
        functools.partial(_gmlp_kernel, chunk=chunk, groups=groups, gdim=gdim),
        grid=(m // tm,),
        in_specs=[
            pl.BlockSpec((tm, w), lambda i: (i, 0)),
            pl.BlockSpec((tm, w), lambda i: (i, 1)),
            pl.BlockSpec((1, w), lambda i: (0, 0)),
            pl.BlockSpec((groups, chunk, chunk), lambda i: (0, 0, 0)),
            pl.BlockSpec((groups, chunk, gdim), lambda i: (0, 0, 0)),
        ],
        out_specs=pl.BlockSpec((tm, w), lambda i: (i, 0)),
        out_shape=jax.ShapeDtypeStruct((m, w), BF16),
        scratch_shapes=[pltpu.VMEM((tm, w), BF16)],
        compiler_params=_cparams("parallel"),
        name="gmlp_gate",
    )(uv, uv, v_norm.reshape(1, w), ws, bs_b)


def _rope_block(blk, c1, c2):
    return blk * c1 + pltpu.roll(blk, LANES // 2, 1) * c2


def _q_proj_kernel(ql_ref, g_ref, w_ref, c1_ref, c2_ref, o_ref, hn_ref, *, heads_per_tile):
    @pl.when(pl.program_id(1) == 0)
    def _():
        x = ql_ref[...]
        hn_ref[...] = ((x * _rms_scale(x)) * g_ref[...]).astype(hn_ref.dtype)

    acc = jnp.dot(hn_ref[...], w_ref[...], preferred_element_type=F32)
    c1, c2 = c1_ref[...], c2_ref[...]
    for h in range(heads_per_tile):
        lo = h * HEAD_PAD
        o_ref[:, lo:lo + LANES] = (acc[:, lo:lo + LANES] * QK_LOG2_SCALE).astype(o_ref.dtype)
        roped = _rope_block(acc[:, lo + LANES:lo + HEAD_PAD], c1, c2)
        o_ref[:, lo + LANES:lo + HEAD_PAD] = (roped * QK_LOG2_SCALE).astype(o_ref.dtype)


def q_proj(lat, q_norm, w_uq_p, c1, c2, q_rank, seq):
    m = lat.shape[0]
    n = w_uq_p.shape[1]
    tm = _tile(seq, 512)
    tn = _tile(n, 4 * HEAD_PAD)
    pos_blocks = seq // tm
    return pl.pallas_call(
        functools.partial(_q_proj_kernel, heads_per_tile=tn // HEAD_PAD),
        grid=(m // tm, n // tn),
        in_specs=[
            pl.BlockSpec((tm, q_rank), lambda i, j: (i, 0)),
            pl.BlockSpec((1, q_rank), lambda i, j: (0, 0)),
            pl.BlockSpec((q_rank, tn), lambda i, j: (0, j)),
            pl.BlockSpec((tm, LANES), lambda i, j: (i % pos_blocks, 0)),
            pl.BlockSpec((tm, LANES), lambda i, j: (i % pos_blocks, 0)),
        ],
        out_specs=pl.BlockSpec((tm, tn), lambda i, j: (i, j)),
        out_shape=jax.ShapeDtypeStruct((m, n), BF16),
        scratch_shapes=[pltpu.VMEM((tm, q_rank), BF16)],
        compiler_params=_cparams("parallel", "arbitrary"),
        name="q_proj",
    )(lat, q_norm.reshape(1, q_rank), w_uq_p, c1, c2)


def _kv_proj_kernel(kvl_ref, kr_ref, g_ref, wk_ref, wv_ref, c1_ref, c2_ref, k_ref, v_ref, *, heads):
    x = kvl_ref[...]
    hn = ((x * _rms_scale(x)) * g_ref[...]).astype(BF16)
    kn = jnp.dot(hn, wk_ref[...], preferred_element_type=F32)
    v_ref[...] = jnp.dot(hn, wv_ref[...], preferred_element_type=F32).astype(v_ref.dtype)
    kr = _rope_block(kr_ref[...], c1_ref[...], c2_ref[...]).astype(k_ref.dtype)
    for h in range(heads):
        lo = h * HEAD_PAD
        k_ref[:, lo:lo + LANES] = kn[:, h * QK_NOPE_DIM:(h + 1) * QK_NOPE_DIM].astype(k_ref.dtype)
        k_ref[:, lo + LANES:lo + HEAD_PAD] = kr


def kv_proj(lat, kv_norm, w_k, w_v, c1, c2, q_rank, kv_rank, heads, seq):
    m = lat.shape[0]
    tm = _tile(seq, 512)
    pos_blocks = seq // tm
    kv_blk = q_rank // kv_rank
    kr_blk = (q_rank + kv_rank) // LANES
    return pl.pallas_call(
        functools.partial(_kv_proj_kernel, heads=heads),
        grid=(m // tm,),
        in_specs=[
            pl.BlockSpec((tm, kv_rank), lambda i: (i, kv_blk)),
            pl.BlockSpec((tm, LANES), lambda i: (i, kr_blk)),
            pl.BlockSpec((1, kv_rank), lambda i: (0, 0)),
            pl.BlockSpec(w_k.shape, lambda i: (0, 0)),
            pl.BlockSpec(w_v.shape, lambda i: (0, 0)),
            pl.BlockSpec((tm, LANES), lambda i: (i % pos_blocks, 0)),
            pl.BlockSpec((tm, LANES), lambda i: (i % pos_blocks, 0)),
        ],
        out_specs=[
            pl.BlockSpec((tm, heads * HEAD_PAD), lambda i: (i, 0)),
            pl.BlockSpec((tm, heads * V_HEAD_DIM), lambda i: (i, 0)),
        ],
        out_shape=[
            jax.ShapeDtypeStruct((m, heads * HEAD_PAD), BF16),
            jax.ShapeDtypeStruct((m, heads * V_HEAD_DIM), BF16),
        ],
        compiler_params=_cparams("parallel"),
        name="kv_proj",
    )(lat, lat, kv_norm.reshape(1, kv_rank), w_k, w_v, c1, c2)


def _attn_kernel(q_ref, k_ref, v_ref, o_ref, *, tq, tk, unroll):
    seq = k_ref.shape[0]

    def q_body(qi, _):
        q0 = pl.multiple_of(qi * tq, tq)
        q = q_ref[pl.ds(q0, tq), :]

        def kv_body(ki, carry):
            m, l, acc = carry
            k0 = pl.multiple_of(ki * tk, tk)
            k = k_ref[pl.ds(k0, tk), :]
            s = lax.dot_general(q, k, (((1,), (1,)), ((), ())), preferred_element_type=F32)
            m_new = jnp.maximum(m, jnp.max(s, axis=-1, keepdims=True))
            p = jnp.exp2(s - m_new)
            alpha = jnp.exp2(m - m_new)
            l = alpha * l + jnp.sum(p, axis=-1, keepdims=True)
            pv = jnp.dot(p.astype(BF16), v_ref[pl.ds(k0, tk), :], preferred_element_type=F32)
            return m_new, l, alpha * acc + pv

        init = (jnp.full((tq, 1), -jnp.inf, F32), jnp.zeros((tq, 1), F32), jnp.zeros((tq, V_HEAD_DIM), F32))
        _, l, acc = lax.fori_loop(0, seq // tk, kv_body, init, unroll=unroll)
        o_ref[pl.ds(q0, tq), :] = (acc / l).astype(o_ref.dtype)
        return 0

    lax.fori_loop(0, seq // tq, q_body, 0)


def attention(q, k, v, batch, seq, heads):
    m = q.shape[0]
    tq, tk = _tile(seq, 1024), _tile(seq, 1024)
    return pl.pallas_call(
        functools.partial(_attn_kernel, tq=tq, tk=tk, unroll=2),
        grid=(batch, heads),
        in_specs=[
            pl.BlockSpec((seq, HEAD_PAD), lambda b, h: (b, h)),
            pl.BlockSpec((seq, HEAD_PAD), lambda b, h: (b, h)),
            pl.BlockSpec((seq, V_HEAD_DIM), lambda b, h: (b, h)),
        ],
        out_specs=pl.BlockSpec((seq, V_HEAD_DIM), lambda b, h: (b, h)),
        out_shape=jax.ShapeDtypeStruct((m, heads * V_HEAD_DIM), BF16),
        compiler_params=_cparams("parallel", "parallel"),
        name="attention",
    )(q, k, v)


def _merge_kernel(a_ref, b_ref, wa_ref, wb_ref, ga_ref, gb_ref, o_ref):
    ya = jnp.dot(a_ref[...], wa_ref[...], preferred_element_type=F32)
    yb = jnp.dot(b_ref[...], wb_ref[...], preferred_element_type=F32)
    o_ref[...] = (_sigmoid(ga_ref[...]) * ya + _sigmoid(gb_ref[...]) * yb).astype(o_ref.dtype)


def gated_merge(a, b, wa, wb, gates):
    m, ka = a.shape
    kb = b.shape[1]
    n = wa.shape[1]
    tm, tn = _tile(m, 1024), _tile(n, 512)
    nj = n // tn
    return pl.pallas_call(
        _merge_kernel,
        grid=(m // tm, nj),
        in_specs=[
            pl.BlockSpec((tm, ka), lambda i, j: (i, 0)),
            pl.BlockSpec((tm, kb), lambda i, j: (i, 0)),
            pl.BlockSpec((ka, tn), lambda i, j: (0, j)),
            pl.BlockSpec((kb, tn), lambda i, j: (0, j)),
            pl.BlockSpec((tm, tn), lambda i, j: (i, j)),
            pl.BlockSpec((tm, tn), lambda i, j: (i, j + nj)),
        ],
        out_specs=pl.BlockSpec((tm, tn), lambda i, j: (i, j)),
        out_shape=jax.ShapeDtypeStruct((m, n), BF16),
        compiler_params=_cparams("parallel", "parallel"),
        name="gated_merge",
    )(a, b, wa, wb, gates, gates)


def _ple_kernel(h_ref, wg_ref, p_ref, wp_ref, o_ref):
    g = jnp.dot(h_ref[...], wg_ref[...], preferred_element_type=F32)
    pp = jnp.dot(p_ref[...].astype(BF16), wp_ref[...], preferred_element_type=F32)
    o_ref[...] = _sigmoid(g) * pp


def ple_gate(h, wg, p, wp):
    m, k = h.shape
    kp = p.shape[1]
    n = wg.shape[1]
    tm, tn = _tile(m, 1024), _tile(n, 512)
    return pl.pallas_call(
        _ple_kernel,
        grid=(m // tm, n // tn),
        in_specs=[
            pl.BlockSpec((tm, k), lambda i, j: (i, 0)),
            pl.BlockSpec((k, tn), lambda i, j: (0, j)),
            pl.BlockSpec((tm, kp), lambda i, j: (i, 0)),
            pl.BlockSpec((kp, tn), lambda i, j: (0, j)),
        ],
        out_specs=pl.BlockSpec((tm, tn), lambda i, j: (i, j)),
        out_shape=jax.ShapeDtypeStruct((m, n), F32),
        compiler_params=_cparams("parallel", "parallel"),
        name="ple_gate",
    )(h, wg, p, wp)


def _final_kernel(x_ref, t_ref, gp_ref, gf_ref, o_ref):
    t = t_ref[...]
    x = x_ref[...] + (t * _rms_scale(t)) * gp_ref[...]
    o_ref[...] = (x * _rms_scale(x)) * gf_ref[...]


def ple_add_final_norm(x, t, post_g, final_g):
    m, d = x.shape
    tm = _tile(m, 256)
    row = pl.BlockSpec((tm, d), lambda i: (i, 0))
    vec = pl.BlockSpec((1, d), lambda i: (0, 0))
    return pl.pallas_call(
        _final_kernel,
        grid=(m // tm,),
        in_specs=[row, row, vec, vec],
        out_specs=row,
        out_shape=jax.ShapeDtypeStruct((m, d), F32),
        compiler_params=_cparams("parallel"),
        name="ple_add_final_norm",
    )(x, t, post_g.reshape(1, d), final_g.reshape(1, d))


def _rope_tables(seq):
    inv = 1.0 / (ROPE_THETA ** (jnp.arange(0, QK_ROPE_DIM, 2, dtype=F32) / QK_ROPE_DIM))
    ang = jnp.arange(seq, dtype=F32)[:, None] * inv[None, :]
    cos, sin = jnp.cos(ang), jnp.sin(ang)
    zero = jnp.zeros((seq, LANES - QK_ROPE_DIM), F32)
    return jnp.concatenate([cos, cos, zero], axis=1), jnp.concatenate([-sin, sin, zero], axis=1)


def _rotate_half_cols(w):
    half = QK_ROPE_DIM // 2
    return jnp.concatenate([w[..., half:], w[..., :half]], axis=-1)


def _prepare_weights(ffn1_wg, ffn1_wu, ffn1_wd, w_in, gm_ws, gm_bs, w_out_a, w_uq, w_ukv, w_out_b, w_out,
                     ffn2_wg, ffn2_wu, ffn2_wd, w_ple_gate, w_ple_proj, q_rank, kv_rank, gm_width):
    def up(w):
        return w.astype(BF16)

    down = up

    o_q = 2 * gm_width
    o_kv = o_q + q_rank
    o_kr = o_kv + kv_rank
    o_gate = o_kr + QK_ROPE_DIM
    w_kr = w_in[:, o_kr:o_gate]
    w_lat = jnp.concatenate([w_in[:, o_q:o_kr], w_kr, _rotate_half_cols(w_kr)], axis=1)

    heads = w_uq.shape[1] // (QK_NOPE_DIM + QK_ROPE_DIM)
    uq = w_uq.reshape(q_rank, heads, QK_NOPE_DIM + QK_ROPE_DIM)
    uq_rope = uq[:, :, QK_NOPE_DIM:]
    w_uq_p = jnp.concatenate([uq[:, :, :QK_NOPE_DIM], uq_rope, _rotate_half_cols(uq_rope)], axis=2)
    ukv = w_ukv.reshape(kv_rank, heads, QK_NOPE_DIM + V_HEAD_DIM)

    gdim = gm_width // gm_ws.shape[0]
    return dict(
        ffn1=(up(ffn1_wg), up(ffn1_wu), down(ffn1_wd)),
        ffn2=(up(ffn2_wg), up(ffn2_wu), down(ffn2_wd)),
        w_uv=w_in[:, :o_q].astype(BF16),
        w_lat=w_lat.astype(BF16),
        w_gates=w_in[:, o_gate:].astype(BF16),
        ws=gm_ws.astype(BF16),
        bs_b=jnp.broadcast_to(gm_bs[:, :, None], gm_bs.shape + (gdim,)).astype(F32),
        w_out_a=w_out_a.astype(BF16),
        w_uq_p=w_uq_p.reshape(q_rank, heads * HEAD_PAD).astype(BF16),
        w_k=ukv[:, :, :QK_NOPE_DIM].reshape(kv_rank, heads * QK_NOPE_DIM).astype(BF16),
        w_v=ukv[:, :, QK_NOPE_DIM:].reshape(kv_rank, heads * V_HEAD_DIM).astype(BF16),
        w_out_b=w_out_b.astype(BF16),
        w_out=w_out.astype(BF16),
        w_ple_gate=w_ple_gate.astype(BF16),
        w_ple_proj=w_ple_proj.astype(BF16),
        heads=heads,
    )


def _trunk(x3d, p3d, w, norms, q_rank, kv_rank):
    batch, seq, d = x3d.shape
    x = x3d.reshape(batch * seq, d)
    p = p3d.reshape(batch * seq, p3d.shape[-1])
    heads = w["heads"]

    x = ffn_half_step(x, norms["ffn1"], *w["ffn1"])

    n = rmsnorm_cast(x, norms["mix"])
    uv = matmul(n, w["w_uv"], F32, 1024, 512, "in_proj_uv")
    lat = matmul(n, w["w_lat"], F32, 512, w["w_lat"].shape[1], "in_proj_lat")
    gates = matmul(n, w["w_gates"], F32, 1024, 512, "in_proj_gates")

    gm = gmlp_gate(uv, norms["gm_v"], w["ws"], w["bs_b"])

    c1, c2 = _rope_tables(seq)
    q = q_proj(lat, norms["q"], w["w_uq_p"], c1, c2, q_rank, seq)
    k, v = kv_proj(lat, norms["kv"], w["w_k"], w["w_v"], c1, c2, q_rank, kv_rank, heads, seq)
    o = attention(q, k, v, batch, seq, heads)

    merged = gated_merge(gm, o, w["w_out_a"], w["w_out_b"], gates)
    x = matmul_residual(merged, w["w_out"], x, 1.0, 1024, 512, "out_proj")

    x = ffn_half_step(x, norms["ffn2"], *w["ffn2"])

    h = rmsnorm_cast(x, norms["ple_gate"])
    t = ple_gate(h, w["w_ple_gate"], p, w["w_ple_proj"])
    y = ple_add_final_norm(x, t, norms["ple_post"], norms["final"])
    return y.reshape(batch, seq, d)


def kernel(x_prompt, x_sample, p_prompt, p_sample, ffn1_norm, ffn1_wg, ffn1_wu, ffn1_wd, mix_norm, w_in, gm_v_norm, gm_ws, gm_bs, w_out_a, q_norm, w_uq, kv_norm, w_ukv, w_out_b, w_out, ffn2_norm, ffn2_wg, ffn2_wu, ffn2_wd, ple_gate_norm, w_ple_gate, w_ple_proj, ple_post_norm, final_norm):
    depth = ffn1_norm.shape[0]
    q_rank = q_norm.shape[1]
    kv_rank = kv_norm.shape[1]
    gm_width = gm_v_norm.shape[1]
    assert q_rank % kv_rank == 0 and (q_rank + kv_rank) % LANES == 0

    xs = [x_prompt, x_sample]
    for l in range(depth):
        w = _prepare_weights(ffn1_wg[l], ffn1_wu[l], ffn1_wd[l], w_in[l], gm_ws[l], gm_bs[l], w_out_a[l], w_uq[l],
                             w_ukv[l], w_out_b[l], w_out[l], ffn2_wg[l], ffn2_wu[l], ffn2_wd[l], w_ple_gate[l],
                             w_ple_proj[l], q_rank, kv_rank, gm_width)
        norms = dict(ffn1=ffn1_norm[l], mix=mix_norm[l], gm_v=gm_v_norm[l], q=q_norm[l], kv=kv_norm[l],
                     ffn2=ffn2_norm[l], ple_gate=ple_gate_norm[l], ple_post=ple_post_norm[l], final=final_norm)
        assert depth == 1
        xs = [_trunk(x, p[l], w, norms, q_rank, kv_rank) for x, p in zip(xs, [p_prompt, p_sample])]
    return tuple(xs)
```

```python
import functools

import jax
import jax.numpy as jnp
import numpy as np
from jax import lax
from jax.experimental import pallas as pl
from jax.experimental.pallas import tpu as pltpu

F32 = jnp.float32
BF16 = jnp.bfloat16

RMS_EPS = 1e-6
ROPE_THETA = 10000.0
QK_NOPE_DIM = 128
QK_ROPE_DIM = 64
V_HEAD_DIM = 128
LANES = 128
HEAD_PAD = 2 * LANES
QK_LOG2_SCALE = float((QK_NOPE_DIM + QK_ROPE_DIM) ** -0.5 * np.log2(np.e))
ATTN_UNROLL_CHUNKS = 8
VMEM_LIMIT = 56 * 1024 * 1024


def _cparams(*sem):
    return pltpu.CompilerParams(dimension_semantics=sem, vmem_limit_bytes=VMEM_LIMIT)


def _tile(n, want):
    if n <= want:
        return n
    t = want
    while n % t:
        t //= 2
    return t


def _sigmoid(x):
    return 1.0 / (1.0 + jnp.exp(-x))


def _gelu_tanh(x):
    c = np.float32(np.sqrt(2.0 / np.pi))
    return x * (0.5 * (1.0 + jnp.tanh(c * (x + 0.044715 * (x * x * x)))))


def _rms_scale(x):
    return lax.rsqrt(jnp.mean(x * x, axis=-1, keepdims=True) + RMS_EPS)


def _rmsnorm_cast_kernel(x_ref, g_ref, o_ref):
    x = x_ref[...]
    o_ref[...] = ((x * _rms_scale(x)) * g_ref[...]).astype(o_ref.dtype)


def rmsnorm_cast(x, g):
    m, d = x.shape
    tm = _tile(m, 256)
    return pl.pallas_call(
        _rmsnorm_cast_kernel,
        grid=(m // tm,),
        in_specs=[pl.BlockSpec((tm, d), lambda i: (i, 0)), pl.BlockSpec((1, d), lambda i: (0, 0))],
        out_specs=pl.BlockSpec((tm, d), lambda i: (i, 0)),
        out_shape=jax.ShapeDtypeStruct((m, d), BF16),
        compiler_params=_cparams("parallel"),
        name="rmsnorm_cast",
    )(x, g.reshape(1, d))


def _mm_kernel(a_ref, w_ref, o_ref):
    o_ref[...] = jnp.dot(a_ref[...], w_ref[...], preferred_element_type=F32).astype(o_ref.dtype)


def matmul(a, w, out_dtype, tm, tn, name):
    m, k = a.shape
    n = w.shape[1]
    tm, tn = _tile(m, tm), _tile(n, tn)
    return pl.pallas_call(
        _mm_kernel,
        grid=(m // tm, n // tn),
        in_specs=[pl.BlockSpec((tm, k), lambda i, j: (i, 0)), pl.BlockSpec((k, tn), lambda i, j: (0, j))],
        out_specs=pl.BlockSpec((tm, tn), lambda i, j: (i, j)),
        out_shape=jax.ShapeDtypeStruct((m, n), out_dtype),
        compiler_params=_cparams("parallel", "parallel"),
        name=name,
    )(a, w)


def _mm_residual_kernel(a_ref, w_ref, x_ref, o_ref, *, scale):
    acc = jnp.dot(a_ref[...], w_ref[...], preferred_element_type=F32)
    o_ref[...] = x_ref[...] + (acc if scale == 1.0 else scale * acc)


def matmul_residual(a, w, x, scale, tm, tn, name):
    m, k = a.shape
    n = w.shape[1]
    tm, tn = _tile(m, tm), _tile(n, tn)
    return pl.pallas_call(
        functools.partial(_mm_residual_kernel, scale=scale),
        grid=(m // tm, n // tn),
        in_specs=[
            pl.BlockSpec((tm, k), lambda i, j: (i, 0)),
            pl.BlockSpec((k, tn), lambda i, j: (0, j)),
            pl.BlockSpec((tm, tn), lambda i, j: (i, j)),
        ],
        out_specs=pl.BlockSpec((tm, tn), lambda i, j: (i, j)),
        out_shape=jax.ShapeDtypeStruct((m, n), F32),
        compiler_params=_cparams("parallel", "parallel"),
        name=name,
    )(a, w, x)


def _ffn_up_kernel(h_ref, wg_ref, wu_ref, o_ref):
    h = h_ref[...]
    g = jnp.dot(h, wg_ref[...], preferred_element_type=F32)
    u = jnp.dot(h, wu_ref[...], preferred_element_type=F32)
    o_ref[...] = ((g * _sigmoid(g)) * u).astype(o_ref.dtype)


def ffn_up(h, wg, wu):
    m, k = h.shape
    n = wg.shape[1]
    tm, tn = _tile(m, 2048), _tile(n, 256)
    return pl.pallas_call(
        _ffn_up_kernel,
        grid=(m // tm, n // tn),
        in_specs=[
            pl.BlockSpec((tm, k), lambda i, j: (i, 0)),
            pl.BlockSpec((k, tn), lambda i, j: (0, j)),
            pl.BlockSpec((k, tn), lambda i, j: (0, j)),
        ],
        out_specs=pl.BlockSpec((tm, tn), lambda i, j: (i, j)),
        out_shape=jax.ShapeDtypeStruct((m, n), BF16),
        compiler_params=_cparams("parallel", "parallel"),
        name="ffn_up",
    )(h, wg, wu)


def ffn_half_step(x, norm_g, wg, wu, wd):
    h = rmsnorm_cast(x, norm_g)
    a = ffn_up(h, wg, wu)
    return matmul_residual(a, wd, x, 0.5, 512, 512, "ffn_down")


def _gmlp_kernel(u_ref, v_ref, vg_ref, ws_ref, bs_ref, o_ref, vn_ref, *, chunk, groups, gdim):
    v = _gelu_tanh(v_ref[...])
    vn_ref[...] = ((v * _rms_scale(v)) * vg_ref[...]).astype(vn_ref.dtype)
    tm = u_ref.shape[0]
    for c in range(tm // chunk):
        rows = slice(c * chunk, (c + 1) * chunk)
        for g in range(groups):
            cols = slice(g * gdim, (g + 1) * gdim)
            mixed = jnp.dot(ws_ref[g], vn_ref[rows, cols], preferred_element_type=F32) + bs_ref[g]
            o_ref[rows, cols] = (_gelu_tanh(u_ref[rows, cols]) * mixed).astype(o_ref.dtype)


def gmlp_gate(uv, v_norm, ws, bs_b):
    m = uv.shape[0]
    groups, chunk, _ = ws.shape
    gdim = bs_b.shape[2]
    w = groups * gdim
    tm = _tile(m, 2 * chunk)
    return pl.pallas_call(
        functools.partial(_gmlp_kernel, chunk=chunk, groups=groups, gdim=gdim),
        grid=(m // tm,),
        in_specs=[
            pl.BlockSpec((tm, w), lambda i: (i, 0)),
            pl.BlockSpec((tm, w), lambda i: (i, 1)),
            pl.BlockSpec((1, w), lambda i: (0, 0)),
            pl.BlockSpec((groups, chunk, chunk), lambda i: (0, 0, 0)),
            pl.BlockSpec((groups, chunk, gdim), lambda i: (0, 0, 0)),
        ],
        out_specs=pl.BlockSpec((tm, w), lambda i: (i, 0)),
        out_shape=jax.ShapeDtypeStruct((m, w), BF16),
        scratch_shapes=[pltpu.VMEM((tm, w), BF16)],
        compiler_params=_cparams("parallel"),
        name="gmlp_gate",
    )(uv, uv, v_norm.reshape(1, w), ws, bs_b)


def _rope_block(blk, c1, c2):
    return blk * c1 + pltpu.roll(blk, LANES // 2, 1) * c2


def _q_proj_kernel(ql_ref, g_ref, w_ref, c1_ref, c2_ref, o_ref, hn_ref, *, heads_per_tile):
    @pl.when(pl.program_id(1) == 0)
    def _():
        x = ql_ref[...]
        hn_ref[...] = ((x * _rms_scale(x)) * g_ref[...]).astype(hn_ref.dtype)

    acc = jnp.dot(hn_ref[...], w_ref[...], preferred_element_type=F32)
    c1, c2 = c1_ref[...], c2_ref[...]
    for h in range(heads_per_tile):
        lo = h * HEAD_PAD
        o_ref[:, lo:lo + LANES] = (acc[:, lo:lo + LANES] * QK_LOG2_SCALE).astype(o_ref.dtype)
        roped = _rope_block(acc[:, lo + LANES:lo + HEAD_PAD], c1, c2)
        o_ref[:, lo + LANES:lo + HEAD_PAD] = (roped * QK_LOG2_SCALE).astype(o_ref.dtype)


def q_proj(lat, q_norm, w_uq_p, c1, c2, q_rank, seq):
    m = lat.shape[0]
    n = w_uq_p.shape[1]
    tm = _tile(seq, 512)
    tn = _tile(n, 4 * HEAD_PAD)
    pos_blocks = seq // tm
    return pl.pallas_call(
        functools.partial(_q_proj_kernel, heads_per_tile=tn // HEAD_PAD),
        grid=(m // tm, n // tn),
        in_specs=[
            pl.BlockSpec((tm, q_rank), lambda i, j: (i, 0)),
            pl.BlockSpec((1, q_rank), lambda i, j: (0, 0)),
            pl.BlockSpec((q_rank, tn), lambda i, j: (0, j)),
            pl.BlockSpec((tm, LANES), lambda i, j: (i % pos_blocks, 0)),
            pl.BlockSpec((tm, LANES), lambda i, j: (i % pos_blocks, 0)),
        ],
        out_specs=pl.BlockSpec((tm, tn), lambda i, j: (i, j)),
        out_shape=jax.ShapeDtypeStruct((m, n), BF16),
        scratch_shapes=[pltpu.VMEM((tm, q_rank), BF16)],
        compiler_params=_cparams("parallel", "arbitrary"),
        name="q_proj",
    )(lat, q_norm.reshape(1, q_rank), w_uq_p, c1, c2)


def _kv_proj_kernel(kvl_ref, kr_ref, g_ref, wk_ref, wv_ref, c1_ref, c2_ref, k_ref, v_ref, *, heads):
    x = kvl_ref[...]
    hn = ((x * _rms_scale(x)) * g_ref[...]).astype(BF16)
    kn = jnp.dot(hn, wk_ref[...], preferred_element_type=F32)
    v_ref[...] = jnp.dot(hn, wv_ref[...], preferred_element_type=F32).astype(v_ref.dtype)
    kr = _rope_block(kr_ref[...], c1_ref[...], c2_ref[...]).astype(k_ref.dtype)
    for h in range(heads):
        lo = h * HEAD_PAD
        k_ref[:, lo:lo + LANES] = kn[:, h * QK_NOPE_DIM:(h + 1) * QK_NOPE_DIM].astype(k_ref.dtype)
        k_ref[:, lo + LANES:lo + HEAD_PAD] = kr


def kv_proj(lat, kv_norm, w_k, w_v, c1, c2, q_rank, kv_rank, heads, seq):
    m = lat.shape[0]
    tm = _tile(seq, 512)
    pos_blocks = seq // tm
    kv_blk = q_rank // kv_rank
    kr_blk = (q_rank + kv_rank) // LANES
    return pl.pallas_call(
        functools.partial(_kv_proj_kernel, heads=heads),
        grid=(m // tm,),
        in_specs=[
            pl.BlockSpec((tm, kv_rank), lambda i: (i, kv_blk)),
            pl.BlockSpec((tm, LANES), lambda i: (i, kr_blk)),
            pl.BlockSpec((1, kv_rank), lambda i: (0, 0)),
            pl.BlockSpec(w_k.shape, lambda i: (0, 0)),
            pl.BlockSpec(w_v.shape, lambda i: (0, 0)),
            pl.BlockSpec((tm, LANES), lambda i: (i % pos_blocks, 0)),
            pl.BlockSpec((tm, LANES), lambda i: (i % pos_blocks, 0)),
        ],
        out_specs=[
            pl.BlockSpec((tm, heads * HEAD_PAD), lambda i: (i, 0)),
            pl.BlockSpec((tm, heads * V_HEAD_DIM), lambda i: (i, 0)),
        ],
        out_shape=[
            jax.ShapeDtypeStruct((m, heads * HEAD_PAD), BF16),
            jax.ShapeDtypeStruct((m, heads * V_HEAD_DIM), BF16),
        ],
        compiler_params=_cparams("parallel"),
        name="kv_proj",
    )(lat, lat, kv_norm.reshape(1, kv_rank), w_k, w_v, c1, c2)


def _attn_kernel(q_ref, k_ref, v_ref, o_ref, *, tq, tk):
    seq = k_ref.shape[0]
    nq, nk = seq // tq, seq // tk
    kv_unroll = min(nk, ATTN_UNROLL_CHUNKS)
    q_unroll = max(1, min(nq, ATTN_UNROLL_CHUNKS // kv_unroll))

    def q_body(qi, _):
        q0 = pl.multiple_of(qi * tq, tq)
        q = q_ref[pl.ds(q0, tq), :]

        def kv_body(ki, carry):
            m, l, acc = carry
            k0 = pl.multiple_of(ki * tk, tk)
            k = k_ref[pl.ds(k0, tk), :]
            s = lax.dot_general(q, k, (((1,), (1,)), ((), ())), preferred_element_type=F32)
            m_new = jnp.maximum(m, jnp.max(s, axis=-1, keepdims=True))
            p = jnp.exp2(s - m_new)
            alpha = jnp.exp2(m - m_new)
            l = alpha * l + jnp.sum(p, axis=-1, keepdims=True)
            pv = jnp.dot(p.astype(BF16), v_ref[pl.ds(k0, tk), :], preferred_element_type=F32)
            return m_new, l, alpha * acc + pv

        init = (jnp.full((tq, 1), -jnp.inf, F32), jnp.zeros((tq, 1), F32), jnp.zeros((tq, V_HEAD_DIM), F32))
        _, l, acc = lax.fori_loop(0, nk, kv_body, init, unroll=kv_unroll)
        o_ref[pl.ds(q0, tq), :] = (acc / l).astype(o_ref.dtype)
        return 0

    lax.fori_loop(0, nq, q_body, 0, unroll=q_unroll)


def attention(q, k, v, batch, seq, heads):
    m = q.shape[0]
    tq, tk = _tile(seq, 1024), _tile(seq, 1024)
    return pl.pallas_call(
        functools.partial(_attn_kernel, tq=tq, tk=tk),
        grid=(batch, heads),
        in_specs=[
            pl.BlockSpec((seq, HEAD_PAD), lambda b, h: (b, h)),
            pl.BlockSpec((seq, HEAD_PAD), lambda b, h: (b, h)),
            pl.BlockSpec((seq, V_HEAD_DIM), lambda b, h: (b, h)),
        ],
        out_specs=pl.BlockSpec((seq, V_HEAD_DIM), lambda b, h: (b, h)),
        out_shape=jax.ShapeDtypeStruct((m, heads * V_HEAD_DIM), BF16),
        compiler_params=_cparams("parallel", "parallel"),
        name="attention",
    )(q, k, v)


def _merge_kernel(a_ref, b_ref, wa_ref, wb_ref, ga_ref, gb_ref, o_ref):
    ya = jnp.dot(a_ref[...], wa_ref[...], preferred_element_type=F32)
    yb = jnp.dot(b_ref[...], wb_ref[...], preferred_element_type=F32)
    o_ref[...] = (_sigmoid(ga_ref[...]) * ya + _sigmoid(gb_ref[...]) * yb).astype(o_ref.dtype)


def gated_merge(a, b, wa, wb, gates):
    m, ka = a.shape
    kb = b.shape[1]
    n = wa.shape[1]
    tm, tn = _tile(m, 1024), _tile(n, 512)
    nj = n // tn
    return pl.pallas_call(
        _merge_kernel,
        grid=(m // tm, nj),
        in_specs=[
            pl.BlockSpec((tm, ka), lambda i, j: (i, 0)),
            pl.BlockSpec((tm, kb), lambda i, j: (i, 0)),
            pl.BlockSpec((ka, tn), lambda i, j: (0, j)),
            pl.BlockSpec((kb, tn), lambda i, j: (0, j)),
            pl.BlockSpec((tm, tn), lambda i, j: (i, j)),
            pl.BlockSpec((tm, tn), lambda i, j: (i, j + nj)),
        ],
        out_specs=pl.BlockSpec((tm, tn), lambda i, j: (i, j)),
        out_shape=jax.ShapeDtypeStruct((m, n), BF16),
        compiler_params=_cparams("parallel", "parallel"),
        name="gated_merge",
    )(a, b, wa, wb, gates, gates)


def _ple_final_kernel(x_ref, gg_ref, wg_ref, p_ref, wp_ref, gp_ref, gf_ref, o_ref, h_ref, t_ref, *, nj, tn):
    j = pl.program_id(1)

    @pl.when(j == 0)
    def _():
        x = x_ref[...]
        h_ref[...] = ((x * _rms_scale(x)) * gg_ref[...]).astype(h_ref.dtype)

    g = jnp.dot(h_ref[...], wg_ref[...], preferred_element_type=F32)
    pp = jnp.dot(p_ref[...].astype(BF16), wp_ref[...], preferred_element_type=F32)
    t_ref[j] = _sigmoid(g) * pp

    @pl.when(j == nj - 1)
    def _():
        d = nj * tn
        cols = [slice(c * tn, (c + 1) * tn) for c in range(nj)]
        ssq = sum(jnp.sum(t_ref[c] * t_ref[c], axis=-1, keepdims=True) for c in range(nj))
        t_scale = lax.rsqrt(ssq / d + RMS_EPS)
        ssq = jnp.zeros_like(ssq)
        for c in range(nj):
            xc = x_ref[:, cols[c]] + (t_ref[c] * t_scale) * gp_ref[:, cols[c]]
            t_ref[c] = xc
            ssq = ssq + jnp.sum(xc * xc, axis=-1, keepdims=True)
        x_scale = lax.rsqrt(ssq / d + RMS_EPS)
        for c in range(nj):
            o_ref[:, cols[c]] = (t_ref[c] * x_scale) * gf_ref[:, cols[c]]


def ple_final(x, gate_g, wg, p, wp, post_g, final_g):
    m, d = x.shape
    kp = p.shape[1]
    tm, tn = _tile(m, 512), _tile(d, 512)
    nj = d // tn
    row = pl.BlockSpec((tm, d), lambda i, j: (i, 0))
    vec = pl.BlockSpec((1, d), lambda i, j: (0, 0))
    return pl.pallas_call(
        functools.partial(_ple_final_kernel, nj=nj, tn=tn),
        grid=(m // tm, nj),
        in_specs=[
            pl.BlockSpec((tm, d), lambda i, j: (i, 0), pipeline_mode=pl.Buffered(1)),
            vec,
            pl.BlockSpec((d, tn), lambda i, j: (0, j)),
            pl.BlockSpec((tm, kp), lambda i, j: (i, 0)),
            pl.BlockSpec((kp, tn), lambda i, j: (0, j)),
            vec,
            vec,
        ],
        out_specs=row,
        out_shape=jax.ShapeDtypeStruct((m, d), F32),
        scratch_shapes=[pltpu.VMEM((tm, d), BF16), pltpu.VMEM((nj, tm, tn), F32)],
        compiler_params=_cparams("parallel", "arbitrary"),
        name="ple_final",
    )(x, gate_g.reshape(1, d), wg, p, wp, post_g.reshape(1, d), final_g.reshape(1, d))


def _rope_tables(seq):
    inv = 1.0 / (ROPE_THETA ** (jnp.arange(0, QK_ROPE_DIM, 2, dtype=F32) / QK_ROPE_DIM))
    ang = jnp.arange(seq, dtype=F32)[:, None] * inv[None, :]
    cos, sin = jnp.cos(ang), jnp.sin(ang)
    zero = jnp.zeros((seq, LANES - QK_ROPE_DIM), F32)
    return jnp.concatenate([cos, cos, zero], axis=1), jnp.concatenate([-sin, sin, zero], axis=1)


def _rotate_half_cols(w):
    half = QK_ROPE_DIM // 2
    return jnp.concatenate([w[..., half:], w[..., :half]], axis=-1)


def _prepare_weights(ffn1_wg, ffn1_wu, ffn1_wd, w_in, gm_ws, gm_bs, w_out_a, w_uq, w_ukv, w_out_b, w_out,
                     ffn2_wg, ffn2_wu, ffn2_wd, w_ple_gate, w_ple_proj, q_rank, kv_rank, gm_width):
    def up(w):
        return w.astype(BF16)

    down = up

    o_q = 2 * gm_width
    o_kv = o_q + q_rank
    o_kr = o_kv + kv_rank
    o_gate = o_kr + QK_ROPE_DIM
    w_kr = w_in[:, o_kr:o_gate]
    w_lat = jnp.concatenate([w_in[:, o_q:o_kr], w_kr, _rotate_half_cols(w_kr)], axis=1)

    heads = w_uq.shape[1] // (QK_NOPE_DIM + QK_ROPE_DIM)
    uq = w_uq.reshape(q_rank, heads, QK_NOPE_DIM + QK_ROPE_DIM)
    uq_rope = uq[:, :, QK_NOPE_DIM:]
    w_uq_p = jnp.concatenate([uq[:, :, :QK_NOPE_DIM], uq_rope, _rotate_half_cols(uq_rope)], axis=2)
    ukv = w_ukv.reshape(kv_rank, heads, QK_NOPE_DIM + V_HEAD_DIM)

    gdim = gm_width // gm_ws.shape[0]
    return dict(
        ffn1=(up(ffn1_wg), up(ffn1_wu), down(ffn1_wd)),
        ffn2=(up(ffn2_wg), up(ffn2_wu), down(ffn2_wd)),
        w_uv=w_in[:, :o_q].astype(BF16),
        w_lat=w_lat.astype(BF16),
        w_gates=w_in[:, o_gate:].astype(BF16),
        ws=gm_ws.astype(BF16),
        bs_b=jnp.broadcast_to(gm_bs[:, :, None], gm_bs.shape + (gdim,)).astype(F32),
        w_out_a=w_out_a.astype(BF16),
        w_uq_p=w_uq_p.reshape(q_rank, heads * HEAD_PAD).astype(BF16),
        w_k=ukv[:, :, :QK_NOPE_DIM].reshape(kv_rank, heads * QK_NOPE_DIM).astype(BF16),
        w_v=ukv[:, :, QK_NOPE_DIM:].reshape(kv_rank, heads * V_HEAD_DIM).astype(BF16),
        w_out_b=w_out_b.astype(BF16),
        w_out=w_out.astype(BF16),
        w_ple_gate=w_ple_gate.astype(BF16),
        w_ple_proj=w_ple_proj.astype(BF16),
        heads=heads,
    )


def _trunk(x3d, p3d, w, norms, q_rank, kv_rank):
    batch, seq, d = x3d.shape
    x = x3d.reshape(batch * seq, d)
    p = p3d.reshape(batch * seq, p3d.shape[-1])
    heads = w["heads"]

    x = ffn_half_step(x, norms["ffn1"], *w["ffn1"])

    n = rmsnorm_cast(x, norms["mix"])
    uv = matmul(n, w["w_uv"], F32, 1024, 512, "in_proj_uv")
    lat = matmul(n, w["w_lat"], F32, 512, w["w_lat"].shape[1], "in_proj_lat")
    gates = matmul(n, w["w_gates"], F32, 1024, 512, "in_proj_gates")

    gm = gmlp_gate(uv, norms["gm_v"], w["ws"], w["bs_b"])

    c1, c2 = _rope_tables(seq)
    q = q_proj(lat, norms["q"], w["w_uq_p"], c1, c2, q_rank, seq)
    k, v = kv_proj(lat, norms["kv"], w["w_k"], w["w_v"], c1, c2, q_rank, kv_rank, heads, seq)
    o = attention(q, k, v, batch, seq, heads)

    merged = gated_merge(gm, o, w["w_out_a"], w["w_out_b"], gates)
    x = matmul_residual(merged, w["w_out"], x, 1.0, 1024, 512, "out_proj")

    x = ffn_half_step(x, norms["ffn2"], *w["ffn2"])

    y = ple_final(x, norms["ple_gate"], w["w_ple_gate"], p, w["w_ple_proj"], norms["ple_post"], norms["final"])
    return y.reshape(batch, seq, d)


def kernel(x_prompt, x_sample, p_prompt, p_sample, ffn1_norm, ffn1_wg, ffn1_wu, ffn1_wd, mix_norm, w_in, gm_v_norm, gm_ws, gm_bs, w_out_a, q_norm, w_uq, kv_norm, w_ukv, w_out_b, w_out, ffn2_norm, ffn2_wg, ffn2_wu, ffn2_wd, ple_gate_norm, w_ple_gate, w_ple_proj, ple_post_norm, final_norm):
    depth = ffn1_norm.shape[0]
    q_rank = q_norm.shape[1]
    kv_rank = kv_norm.shape[1]
    gm_width = gm_v_norm.shape[1]
    assert q_rank % kv_rank == 0 and (q_rank + kv_rank) % LANES == 0

    xs = [x_prompt, x_sample]
    for l in range(depth):
        w = _prepare_weights(ffn1_wg[l], ffn1_wu[l], ffn1_wd[l], w_in[l], gm_ws[l], gm_bs[l], w_out_a[l], w_uq[l],
                             w_ukv[l], w_out_b[l], w_out[l], ffn2_wg[l], ffn2_wu[l], ffn2_wd[l], w_ple_gate[l],
                             w_ple_proj[l], q_rank, kv_rank, gm_width)
        norms = dict(ffn1=ffn1_norm[l], mix=mix_norm[l], gm_v=gm_v_norm[l], q=q_norm[l], kv=kv_norm[l],
                     ffn2=ffn2_norm[l], ple_gate=ple_gate_norm[l], ple_post=ple_post_norm[l], final=final_norm)
        assert depth == 1
        xs = [_trunk(x, p[l], w, norms, q_rank, kv_rank) for x, p in zip(xs, [p_prompt, p_sample])]
    return tuple(xs)
```

```python
import functools

import jax
import jax.numpy as jnp
import numpy as np
from jax import lax
from jax.experimental import pallas as pl
from jax.experimental.pallas import tpu as pltpu

F32 = jnp.float32
BF16 = jnp.bfloat16

RMS_EPS = 1e-6
ROPE_THETA = 10000.0
QK_NOPE_DIM = 128
QK_ROPE_DIM = 64
V_HEAD_DIM = 128
LANES = 128
HEAD_PAD = 2 * LANES
QK_LOG2_SCALE = float((QK_NOPE_DIM + QK_ROPE_DIM) ** -0.5 * np.log2(np.e))
ATTN_UNROLL_CHUNKS = 8
VMEM_LIMIT = 56 * 1024 * 1024


def _cparams(*sem):
    return pltpu.CompilerParams(dimension_semantics=sem, vmem_limit_bytes=VMEM_LIMIT)


def _tile(n, want):
    if n <= want:
        return n
    t = want
    while n % t:
        t //= 2
    return t


def _sigmoid(x):
    return 1.0 / (1.0 + jnp.exp(-x))


def _gelu_tanh(x):
    c = np.float32(np.sqrt(2.0 / np.pi))
    return x * (0.5 * (1.0 + jnp.tanh(c * (x + 0.044715 * (x * x * x)))))


def _rms_scale(x):
    return lax.rsqrt(jnp.mean(x * x, axis=-1, keepdims=True) + RMS_EPS)


def _lane_partial_sumsq(x):
    x2 = x * x
    return sum(x2[:, c * LANES:(c + 1) * LANES] for c in range(x.shape[1] // LANES))


def _row_scale(ssq_ref, width):
    return lax.rsqrt(jnp.sum(ssq_ref[...], axis=-1, keepdims=True) / width + RMS_EPS)


def _cast_sumsq_kernel(x_ref, xb_ref, ssq_ref):
    x = x_ref[...]
    xb_ref[...] = x.astype(xb_ref.dtype)
    ssq_ref[...] = _lane_partial_sumsq(x)


def cast_sumsq(x):
    m, d = x.shape
    tm = _tile(m, 256)
    return pl.pallas_call(
        _cast_sumsq_kernel,
        grid=(m // tm,),
        in_specs=[pl.BlockSpec((tm, d), lambda i: (i, 0))],
        out_specs=[pl.BlockSpec((tm, d), lambda i: (i, 0)), pl.BlockSpec((tm, LANES), lambda i: (i, 0))],
        out_shape=[jax.ShapeDtypeStruct((m, d), BF16), jax.ShapeDtypeStruct((m, LANES), F32)],
        compiler_params=_cparams("parallel"),
        name="cast_sumsq",
    )(x)


def _mm_normed_kernel(xb_ref, ssq_ref, w_ref, o_ref, rs_ref):
    @pl.when(pl.program_id(1) == 0)
    def _():
        rs_ref[...] = _row_scale(ssq_ref, xb_ref.shape[1])

    acc = jnp.dot(xb_ref[...], w_ref[...], preferred_element_type=F32)
    o_ref[...] = (acc * rs_ref[...]).astype(o_ref.dtype)


def matmul_normed(xb, ssq, w, out_dtype, tm, tn, name):
    m, k = xb.shape
    n = w.shape[1]
    tm, tn = _tile(m, tm), _tile(n, tn)
    return pl.pallas_call(
        _mm_normed_kernel,
        grid=(m // tm, n // tn),
        in_specs=[
            pl.BlockSpec((tm, k), lambda i, j: (i, 0)),
            pl.BlockSpec((tm, LANES), lambda i, j: (i, 0)),
            pl.BlockSpec((k, tn), lambda i, j: (0, j)),
        ],
        out_specs=pl.BlockSpec((tm, tn), lambda i, j: (i, j)),
        out_shape=jax.ShapeDtypeStruct((m, n), out_dtype),
        scratch_shapes=[pltpu.VMEM((tm, 1), F32)],
        compiler_params=_cparams("parallel", "arbitrary"),
        name=name,
    )(xb, ssq, w)


def _mm_residual_kernel(a_ref, w_ref, x_ref, o_ref, *norm_refs, scale):
    acc = jnp.dot(a_ref[...], w_ref[...], preferred_element_type=F32)
    y = x_ref[...] + (acc if scale == 1.0 else scale * acc)
    o_ref[...] = y
    if norm_refs:
        yb_ref, ssq_ref = norm_refs
        yb_ref[...] = y.astype(yb_ref.dtype)

        @pl.when(pl.program_id(1) == 0)
        def _():
            ssq_ref[...] = jnp.zeros_like(ssq_ref)

        ssq_ref[...] += _lane_partial_sumsq(y)


def matmul_residual(a, w, x, scale, tm, tn, name, emit_norm_inputs=False):
    m, k = a.shape
    n = w.shape[1]
    tm, tn = _tile(m, tm), _tile(n, tn)
    tile = pl.BlockSpec((tm, tn), lambda i, j: (i, j))
    out_specs, out_shape = [tile], [jax.ShapeDtypeStruct((m, n), F32)]
    if emit_norm_inputs:
        out_specs += [tile, pl.BlockSpec((tm, LANES), lambda i, j: (i, 0))]
        out_shape += [jax.ShapeDtypeStruct((m, n), BF16), jax.ShapeDtypeStruct((m, LANES), F32)]
    out = pl.pallas_call(
        functools.partial(_mm_residual_kernel, scale=scale),
        grid=(m // tm, n // tn),
        in_specs=[pl.BlockSpec((tm, k), lambda i, j: (i, 0)), pl.BlockSpec((k, tn), lambda i, j: (0, j)), tile],
        out_specs=out_specs,
        out_shape=out_shape,
        compiler_params=_cparams("parallel", "arbitrary"),
        name=name,
    )(a, w, x)
    return out if emit_norm_inputs else out[0]


def _ffn_up_kernel(xb_ref, ssq_ref, wg_ref, wu_ref, o_ref, rs_ref, *, row_splits):
    @pl.when(pl.program_id(1) == 0)
    def _():
        rs_ref[...] = _row_scale(ssq_ref, xb_ref.shape[1])

    rows = xb_ref.shape[0] // row_splits
    for r in range(row_splits):
        sl = slice(r * rows, (r + 1) * rows)
        xb = xb_ref[sl, :]
        rs = rs_ref[sl, :]
        g = jnp.dot(xb, wg_ref[...], preferred_element_type=F32) * rs
        u = jnp.dot(xb, wu_ref[...], preferred_element_type=F32) * rs
        o_ref[sl, :] = ((g * _sigmoid(g)) * u).astype(o_ref.dtype)


def ffn_up(xb, ssq, wg, wu):
    m, k = xb.shape
    n = wg.shape[1]
    tm, tn = _tile(m, 2048), _tile(n, 256)
    return pl.pallas_call(
        functools.partial(_ffn_up_kernel, row_splits=4 if tm % 1024 == 0 else 1),
        grid=(m // tm, n // tn),
        in_specs=[
            pl.BlockSpec((tm, k), lambda i, j: (i, 0)),
            pl.BlockSpec((tm, LANES), lambda i, j: (i, 0)),
            pl.BlockSpec((k, tn), lambda i, j: (0, j)),
            pl.BlockSpec((k, tn), lambda i, j: (0, j)),
        ],
        out_specs=pl.BlockSpec((tm, tn), lambda i, j: (i, j)),
        out_shape=jax.ShapeDtypeStruct((m, n), BF16),
        scratch_shapes=[pltpu.VMEM((tm, 1), F32)],
        compiler_params=_cparams("parallel", "arbitrary"),
        name="ffn_up",
    )(xb, ssq, wg, wu)


def _gmlp_kernel(u_ref, v_ref, vg_ref, ws_ref, bs_ref, o_ref, vn_ref, *, chunk, groups, gdim):
    v = _gelu_tanh(v_ref[...])
    vn_ref[...] = ((v * _rms_scale(v)) * vg_ref[...]).astype(vn_ref.dtype)
    tm = u_ref.shape[0]
    for c in range(tm // chunk):
        rows = slice(c * chunk, (c + 1) * chunk)
        for g in range(groups):
            cols = slice(g * gdim, (g + 1) * gdim)
            mixed = jnp.dot(ws_ref[g], vn_ref[rows, cols], preferred_element_type=F32) + bs_ref[g]
            o_ref[rows, cols] = (_gelu_tanh(u_ref[rows, cols]) * mixed).astype(o_ref.dtype)


def gmlp_gate(uv, v_norm, ws, bs_b):
    m = uv.shape[0]
    groups, chunk, _ = ws.shape
    gdim = bs_b.shape[2]
    w = groups * gdim
    tm = _tile(m, 2 * chunk)
    return pl.pallas_call(
        functools.partial(_gmlp_kernel, chunk=chunk, groups=groups, gdim=gdim),
        grid=(m // tm,),
        in_specs=[
            pl.BlockSpec((tm, w), lambda i: (i, 0)),
            pl.BlockSpec((tm, w), lambda i: (i, 1)),
            pl.BlockSpec((1, w), lambda i: (0, 0)),
            pl.BlockSpec((groups, chunk, chunk), lambda i: (0, 0, 0)),
            pl.BlockSpec((groups, chunk, gdim), lambda i: (0, 0, 0)),
        ],
        out_specs=pl.BlockSpec((tm, w), lambda i: (i, 0)),
        out_shape=jax.ShapeDtypeStruct((m, w), BF16),
        scratch_shapes=[pltpu.VMEM((tm, w), BF16)],
        compiler_params=_cparams("parallel"),
        name="gmlp_gate",
    )(uv, uv, v_norm.reshape(1, w), ws, bs_b)


def _rope_block(blk, c1, c2):
    return blk * c1 + pltpu.roll(blk, LANES // 2, 1) * c2


def _q_proj_kernel(ql_ref, g_ref, w_ref, c1_ref, c2_ref, o_ref, hn_ref, *, heads_per_tile):
    @pl.when(pl.program_id(1) == 0)
    def _():
        x = ql_ref[...]
        hn_ref[...] = ((x * _rms_scale(x)) * g_ref[...]).astype(hn_ref.dtype)

    acc = jnp.dot(hn_ref[...], w_ref[...], preferred_element_type=F32)
    c1, c2 = c1_ref[...], c2_ref[...]
    for h in range(heads_per_tile):
        lo = h * HEAD_PAD
        o_ref[:, lo:lo + LANES] = (acc[:, lo:lo + LANES] * QK_LOG2_SCALE).astype(o_ref.dtype)
        roped = _rope_block(acc[:, lo + LANES:lo + HEAD_PAD], c1, c2)
        o_ref[:, lo + LANES:lo + HEAD_PAD] = (roped * QK_LOG2_SCALE).astype(o_ref.dtype)


def q_proj(lat, q_norm, w_uq_p, c1, c2, q_rank, seq):
    m = lat.shape[0]
    n = w_uq_p.shape[1]
    tm = _tile(seq, 512)
    tn = _tile(n, 4 * HEAD_PAD)
    pos_blocks = seq // tm
    return pl.pallas_call(
        functools.partial(_q_proj_kernel, heads_per_tile=tn // HEAD_PAD),
        grid=(m // tm, n // tn),
        in_specs=[
            pl.BlockSpec((tm, q_rank), lambda i, j: (i, 0)),
            pl.BlockSpec((1, q_rank), lambda i, j: (0, 0)),
            pl.BlockSpec((q_rank, tn), lambda i, j: (0, j)),
            pl.BlockSpec((tm, LANES), lambda i, j: (i % pos_blocks, 0)),
            pl.BlockSpec((tm, LANES), lambda i, j: (i % pos_blocks, 0)),
        ],
        out_specs=pl.BlockSpec((tm, tn), lambda i, j: (i, j)),
        out_shape=jax.ShapeDtypeStruct((m, n), BF16),
        scratch_shapes=[pltpu.VMEM((tm, q_rank), BF16)],
        compiler_params=_cparams("parallel", "arbitrary"),
        name="q_proj",
    )(lat, q_norm.reshape(1, q_rank), w_uq_p, c1, c2)


def _kv_proj_kernel(kvl_ref, kr_ref, g_ref, wk_ref, wv_ref, c1_ref, c2_ref, k_ref, v_ref, *, heads):
    x = kvl_ref[...]
    hn = ((x * _rms_scale(x)) * g_ref[...]).astype(BF16)
    kn = jnp.dot(hn, wk_ref[...], preferred_element_type=F32)
    v_ref[...] = jnp.dot(hn, wv_ref[...], preferred_element_type=F32).astype(v_ref.dtype)
    kr = _rope_block(kr_ref[...], c1_ref[...], c2_ref[...]).astype(k_ref.dtype)
    for h in range(heads):
        lo = h * HEAD_PAD
        k_ref[:, lo:lo + LANES] = kn[:, h * QK_NOPE_DIM:(h + 1) * QK_NOPE_DIM].astype(k_ref.dtype)
        k_ref[:, lo + LANES:lo + HEAD_PAD] = kr


def kv_proj(lat, kv_norm, w_k, w_v, c1, c2, q_rank, kv_rank, heads, seq):
    m = lat.shape[0]
    tm = _tile(seq, 512)
    pos_blocks = seq // tm
    kv_blk = q_rank // kv_rank
    kr_blk = (q_rank + kv_rank) // LANES
    return pl.pallas_call(
        functools.partial(_kv_proj_kernel, heads=heads),
        grid=(m // tm,),
        in_specs=[
            pl.BlockSpec((tm, kv_rank), lambda i: (i, kv_blk)),
            pl.BlockSpec((tm, LANES), lambda i: (i, kr_blk)),
            pl.BlockSpec((1, kv_rank), lambda i: (0, 0)),
            pl.BlockSpec(w_k.shape, lambda i: (0, 0)),
            pl.BlockSpec(w_v.shape, lambda i: (0, 0)),
            pl.BlockSpec((tm, LANES), lambda i: (i % pos_blocks, 0)),
            pl.BlockSpec((tm, LANES), lambda i: (i % pos_blocks, 0)),
        ],
        out_specs=[
            pl.BlockSpec((tm, heads * HEAD_PAD), lambda i: (i, 0)),
            pl.BlockSpec((tm, heads * V_HEAD_DIM), lambda i: (i, 0)),
        ],
        out_shape=[
            jax.ShapeDtypeStruct((m, heads * HEAD_PAD), BF16),
            jax.ShapeDtypeStruct((m, heads * V_HEAD_DIM), BF16),
        ],
        compiler_params=_cparams("parallel"),
        name="kv_proj",
    )(lat, lat, kv_norm.reshape(1, kv_rank), w_k, w_v, c1, c2)


def _attn_kernel(q_ref, k_ref, v_ref, o_ref, *, tq, tk):
    seq = k_ref.shape[0]
    nq, nk = seq // tq, seq // tk
    kv_unroll = min(nk, ATTN_UNROLL_CHUNKS)
    q_unroll = max(1, min(nq, ATTN_UNROLL_CHUNKS // kv_unroll))

    def q_body(qi, _):
        q0 = pl.multiple_of(qi * tq, tq)
        q = q_ref[pl.ds(q0, tq), :]

        def kv_body(ki, carry):
            m, l, acc = carry
            k0 = pl.multiple_of(ki * tk, tk)
            k = k_ref[pl.ds(k0, tk), :]
            s = lax.dot_general(q, k, (((1,), (1,)), ((), ())), preferred_element_type=F32)
            m_new = jnp.maximum(m, jnp.max(s, axis=-1, keepdims=True))
            p = jnp.exp2(s - m_new)
            alpha = jnp.exp2(m - m_new)
            l = alpha * l + jnp.sum(p, axis=-1, keepdims=True)
            pv = jnp.dot(p.astype(BF16), v_ref[pl.ds(k0, tk), :], preferred_element_type=F32)
            return m_new, l, alpha * acc + pv

        init = (jnp.full((tq, 1), -jnp.inf, F32), jnp.zeros((tq, 1), F32), jnp.zeros((tq, V_HEAD_DIM), F32))
        _, l, acc = lax.fori_loop(0, nk, kv_body, init, unroll=kv_unroll)
        o_ref[pl.ds(q0, tq), :] = (acc / l).astype(o_ref.dtype)
        return 0

    lax.fori_loop(0, nq, q_body, 0, unroll=q_unroll)


def attention(q, k, v, batch, seq, heads):
    m = q.shape[0]
    tq, tk = _tile(seq, 1024), _tile(seq, 1024)
    return pl.pallas_call(
        functools.partial(_attn_kernel, tq=tq, tk=tk),
        grid=(batch, heads),
        in_specs=[
            pl.BlockSpec((seq, HEAD_PAD), lambda b, h: (b, h)),
            pl.BlockSpec((seq, HEAD_PAD), lambda b, h: (b, h)),
            pl.BlockSpec((seq, V_HEAD_DIM), lambda b, h: (b, h)),
        ],
        out_specs=pl.BlockSpec((seq, V_HEAD_DIM), lambda b, h: (b, h)),
        out_shape=jax.ShapeDtypeStruct((m, heads * V_HEAD_DIM), BF16),
        compiler_params=_cparams("parallel", "parallel"),
        name="attention",
    )(q, k, v)


def _merge_kernel(a_ref, b_ref, wa_ref, wb_ref, ga_ref, gb_ref, o_ref):
    ya = jnp.dot(a_ref[...], wa_ref[...], preferred_element_type=F32)
    yb = jnp.dot(b_ref[...], wb_ref[...], preferred_element_type=F32)
    o_ref[...] = (_sigmoid(ga_ref[...]) * ya + _sigmoid(gb_ref[...]) * yb).astype(o_ref.dtype)


def gated_merge(a, b, wa, wb, gates):
    m, ka = a.shape
    kb = b.shape[1]
    n = wa.shape[1]
    tm, tn = _tile(m, 1024), _tile(n, 512)
    nj = n // tn
    return pl.pallas_call(
        _merge_kernel,
        grid=(m // tm, nj),
        in_specs=[
            pl.BlockSpec((tm, ka), lambda i, j: (i, 0)),
            pl.BlockSpec((tm, kb), lambda i, j: (i, 0)),
            pl.BlockSpec((ka, tn), lambda i, j: (0, j)),
            pl.BlockSpec((kb, tn), lambda i, j: (0, j)),
            pl.BlockSpec((tm, tn), lambda i, j: (i, j)),
            pl.BlockSpec((tm, tn), lambda i, j: (i, j + nj)),
        ],
        out_specs=pl.BlockSpec((tm, tn), lambda i, j: (i, j)),
        out_shape=jax.ShapeDtypeStruct((m, n), BF16),
        compiler_params=_cparams("parallel", "parallel"),
        name="gated_merge",
    )(a, b, wa, wb, gates, gates)


def _ple_final_kernel(x_ref, gg_ref, wg_ref, p_ref, wp_ref, gp_ref, gf_ref, o_ref, h_ref, t_ref, *, nj, tn):
    j = pl.program_id(1)

    @pl.when(j == 0)
    def _():
        x = x_ref[...]
        h_ref[...] = ((x * _rms_scale(x)) * gg_ref[...]).astype(h_ref.dtype)

    g = jnp.dot(h_ref[...], wg_ref[...], preferred_element_type=F32)
    pp = jnp.dot(p_ref[...].astype(BF16), wp_ref[...], preferred_element_type=F32)
    t_ref[j] = _sigmoid(g) * pp

    @pl.when(j == nj - 1)
    def _():
        d = nj * tn
        cols = [slice(c * tn, (c + 1) * tn) for c in range(nj)]
        ssq = sum(jnp.sum(t_ref[c] * t_ref[c], axis=-1, keepdims=True) for c in range(nj))
        t_scale = lax.rsqrt(ssq / d + RMS_EPS)
        ssq = jnp.zeros_like(ssq)
        for c in range(nj):
            xc = x_ref[:, cols[c]] + (t_ref[c] * t_scale) * gp_ref[:, cols[c]]
            t_ref[c] = xc
            ssq = ssq + jnp.sum(xc * xc, axis=-1, keepdims=True)
        x_scale = lax.rsqrt(ssq / d + RMS_EPS)
        for c in range(nj):
            o_ref[:, cols[c]] = (t_ref[c] * x_scale) * gf_ref[:, cols[c]]


def ple_final(x, gate_g, wg, p, wp, post_g, final_g):
    m, d = x.shape
    kp = p.shape[1]
    tm, tn = _tile(m, 512), _tile(d, 512)
    nj = d // tn
    row = pl.BlockSpec((tm, d), lambda i, j: (i, 0))
    vec = pl.BlockSpec((1, d), lambda i, j: (0, 0))
    return pl.pallas_call(
        functools.partial(_ple_final_kernel, nj=nj, tn=tn),
        grid=(m // tm, nj),
        in_specs=[
            pl.BlockSpec((tm, d), lambda i, j: (i, 0), pipeline_mode=pl.Buffered(1)),
            vec,
            pl.BlockSpec((d, tn), lambda i, j: (0, j)),
            pl.BlockSpec((tm, kp), lambda i, j: (i, 0)),
            pl.BlockSpec((kp, tn), lambda i, j: (0, j)),
            vec,
            vec,
        ],
        out_specs=row,
        out_shape=jax.ShapeDtypeStruct((m, d), F32),
        scratch_shapes=[pltpu.VMEM((tm, d), BF16), pltpu.VMEM((nj, tm, tn), F32)],
        compiler_params=_cparams("parallel", "arbitrary"),
        name="ple_final",
    )(x, gate_g.reshape(1, d), wg, p, wp, post_g.reshape(1, d), final_g.reshape(1, d))


def _rope_tables(seq):
    inv = 1.0 / (ROPE_THETA ** (jnp.arange(0, QK_ROPE_DIM, 2, dtype=F32) / QK_ROPE_DIM))
    ang = jnp.arange(seq, dtype=F32)[:, None] * inv[None, :]
    cos, sin = jnp.cos(ang), jnp.sin(ang)
    zero = jnp.zeros((seq, LANES - QK_ROPE_DIM), F32)
    return jnp.concatenate([cos, cos, zero], axis=1), jnp.concatenate([-sin, sin, zero], axis=1)


def _rotate_half_cols(w):
    half = QK_ROPE_DIM // 2
    return jnp.concatenate([w[..., half:], w[..., :half]], axis=-1)


def _prepare_weights(ffn1_norm, ffn1_wg, ffn1_wu, ffn1_wd, mix_norm, w_in, gm_ws, gm_bs, w_out_a, w_uq, w_ukv,
                     w_out_b, w_out, ffn2_norm, ffn2_wg, ffn2_wu, ffn2_wd, w_ple_gate, w_ple_proj, q_rank, kv_rank,
                     gm_width):
    def up(w, gain):
        return (gain[:, None] * w).astype(BF16)

    def down(w):
        return w.astype(BF16)

    w_in = mix_norm[:, None] * w_in
    o_q = 2 * gm_width
    o_kv = o_q + q_rank
    o_kr = o_kv + kv_rank
    o_gate = o_kr + QK_ROPE_DIM
    w_kr = w_in[:, o_kr:o_gate]
    w_lat = jnp.concatenate([w_in[:, o_q:o_kr], w_kr, _rotate_half_cols(w_kr)], axis=1)

    heads = w_uq.shape[1] // (QK_NOPE_DIM + QK_ROPE_DIM)
    uq = w_uq.reshape(q_rank, heads, QK_NOPE_DIM + QK_ROPE_DIM)
    uq_rope = uq[:, :, QK_NOPE_DIM:]
    w_uq_p = jnp.concatenate([uq[:, :, :QK_NOPE_DIM], uq_rope, _rotate_half_cols(uq_rope)], axis=2)
    ukv = w_ukv.reshape(kv_rank, heads, QK_NOPE_DIM + V_HEAD_DIM)

    gdim = gm_width // gm_ws.shape[0]
    return dict(
        ffn1=(up(ffn1_wg, ffn1_norm), up(ffn1_wu, ffn1_norm), down(ffn1_wd)),
        ffn2=(up(ffn2_wg, ffn2_norm), up(ffn2_wu, ffn2_norm), down(ffn2_wd)),
        w_uv=w_in[:, :o_q].astype(BF16),
        w_lat=w_lat.astype(BF16),
        w_gates=w_in[:, o_gate:].astype(BF16),
        ws=gm_ws.astype(BF16),
        bs_b=jnp.broadcast_to(gm_bs[:, :, None], gm_bs.shape + (gdim,)).astype(F32),
        w_out_a=w_out_a.astype(BF16),
        w_uq_p=w_uq_p.reshape(q_rank, heads * HEAD_PAD).astype(BF16),
        w_k=ukv[:, :, :QK_NOPE_DIM].reshape(kv_rank, heads * QK_NOPE_DIM).astype(BF16),
        w_v=ukv[:, :, QK_NOPE_DIM:].reshape(kv_rank, heads * V_HEAD_DIM).astype(BF16),
        w_out_b=w_out_b.astype(BF16),
        w_out=w_out.astype(BF16),
        w_ple_gate=w_ple_gate.astype(BF16),
        w_ple_proj=w_ple_proj.astype(BF16),
        heads=heads,
    )


def _trunk(x3d, p3d, w, norms, q_rank, kv_rank):
    batch, seq, d = x3d.shape
    x = x3d.reshape(batch * seq, d)
    p = p3d.reshape(batch * seq, p3d.shape[-1])
    heads = w["heads"]

    wg, wu, wd = w["ffn1"]
    xb, ssq = cast_sumsq(x)
    x, xb, ssq = matmul_residual(ffn_up(xb, ssq, wg, wu), wd, x, 0.5, 512, 512, "ffn_down", emit_norm_inputs=True)

    uv = matmul_normed(xb, ssq, w["w_uv"], F32, 1024, 512, "in_proj_uv")
    lat = matmul_normed(xb, ssq, w["w_lat"], F32, 512, w["w_lat"].shape[1], "in_proj_lat")
    gates = matmul_normed(xb, ssq, w["w_gates"], F32, 1024, 512, "in_proj_gates")

    gm = gmlp_gate(uv, norms["gm_v"], w["ws"], w["bs_b"])

    c1, c2 = _rope_tables(seq)
    q = q_proj(lat, norms["q"], w["w_uq_p"], c1, c2, q_rank, seq)
    k, v = kv_proj(lat, norms["kv"], w["w_k"], w["w_v"], c1, c2, q_rank, kv_rank, heads, seq)
    o = attention(q, k, v, batch, seq, heads)

    merged = gated_merge(gm, o, w["w_out_a"], w["w_out_b"], gates)
    x, xb, ssq = matmul_residual(merged, w["w_out"], x, 1.0, 1024, 512, "out_proj", emit_norm_inputs=True)

    wg, wu, wd = w["ffn2"]
    x = matmul_residual(ffn_up(xb, ssq, wg, wu), wd, x, 0.5, 512, 512, "ffn_down")

    y = ple_final(x, norms["ple_gate"], w["w_ple_gate"], p, w["w_ple_proj"], norms["ple_post"], norms["final"])
    return y.reshape(batch, seq, d)


def kernel(x_prompt, x_sample, p_prompt, p_sample, ffn1_norm, ffn1_wg, ffn1_wu, ffn1_wd, mix_norm, w_in, gm_v_norm, gm_ws, gm_bs, w_out_a, q_norm, w_uq, kv_norm, w_ukv, w_out_b, w_out, ffn2_norm, ffn2_wg, ffn2_wu, ffn2_wd, ple_gate_norm, w_ple_gate, w_ple_proj, ple_post_norm, final_norm):
    depth = ffn1_norm.shape[0]
    q_rank = q_norm.shape[1]
    kv_rank = kv_norm.shape[1]
    gm_width = gm_v_norm.shape[1]
    assert q_rank % kv_rank == 0 and (q_rank + kv_rank) % LANES == 0

    xs = [x_prompt, x_sample]
    for l in range(depth):
        w = _prepare_weights(ffn1_norm[l], ffn1_wg[l], ffn1_wu[l], ffn1_wd[l], mix_norm[l], w_in[l], gm_ws[l],
                             gm_bs[l], w_out_a[l], w_uq[l], w_ukv[l], w_out_b[l], w_out[l], ffn2_norm[l], ffn2_wg[l],
                             ffn2_wu[l], ffn2_wd[l], w_ple_gate[l], w_ple_proj[l], q_rank, kv_rank, gm_width)
        norms = dict(gm_v=gm_v_norm[l], q=q_norm[l], kv=kv_norm[l], ple_gate=ple_gate_norm[l],
                     ple_post=ple_post_norm[l], final=final_norm)
        assert depth == 1
        xs = [_trunk(x, p[l], w, norms, q_rank, kv_rank) for x, p in zip(xs, [p_prompt, p_sample])]
    return tuple(xs)
```

```python
import functools

import jax
import jax.numpy as jnp
import numpy as np
from jax import lax
from jax.experimental import pallas as pl
from jax.experimental.pallas import tpu as pltpu

F32 = jnp.float32
BF16 = jnp.bfloat16

RMS_EPS = 1e-6
ROPE_THETA = 10000.0
QK_NOPE_DIM = 128
QK_ROPE_DIM = 64
V_HEAD_DIM = 128
LANES = 128
HEAD_PAD = 2 * LANES
QK_LOG2_SCALE = float((QK_NOPE_DIM + QK_ROPE_DIM) ** -0.5 * np.log2(np.e))
COL_TILE = 512
ATTN_UNROLL_CHUNKS = 8
VMEM_LIMIT = 56 * 1024 * 1024


def _cparams(*sem):
    return pltpu.CompilerParams(dimension_semantics=sem, vmem_limit_bytes=VMEM_LIMIT)


def _tile(n, want):
    if n <= want:
        return n
    t = want
    while n % t:
        t //= 2
    return t


def _sigmoid(x):
    return 1.0 / (1.0 + jnp.exp(-x))


def _gelu_tanh(x):
    c = np.float32(np.sqrt(2.0 / np.pi))
    return x * (0.5 * (1.0 + jnp.tanh(c * (x + 0.044715 * (x * x * x)))))


def _rms_scale(x):
    return lax.rsqrt(jnp.mean(x * x, axis=-1, keepdims=True) + RMS_EPS)


def _lane_partial_sumsq(x):
    x2 = x * x
    return sum(x2[:, c * LANES:(c + 1) * LANES] for c in range(x.shape[1] // LANES))


def _row_scale(ssq_ref, width):
    return lax.rsqrt(jnp.sum(ssq_ref[...], axis=-1, keepdims=True) / width + RMS_EPS)


def _cast_sumsq_kernel(x_ref, xb_ref, ssq_ref):
    x = x_ref[...]
    xb_ref[...] = x.astype(xb_ref.dtype)
    ssq_ref[...] = _lane_partial_sumsq(x)


def cast_sumsq(x):
    m, d = x.shape
    tm = _tile(m, 256)
    return pl.pallas_call(
        _cast_sumsq_kernel,
        grid=(m // tm,),
        in_specs=[pl.BlockSpec((tm, d), lambda i: (i, 0))],
        out_specs=[pl.BlockSpec((tm, d), lambda i: (i, 0)), pl.BlockSpec((tm, LANES), lambda i: (i, 0))],
        out_shape=[jax.ShapeDtypeStruct((m, d), BF16), jax.ShapeDtypeStruct((m, LANES), F32)],
        compiler_params=_cparams("parallel"),
        name="cast_sumsq",
    )(x)


def _mm_normed_kernel(xb_ref, ssq_ref, w_ref, o_ref, rs_ref):
    @pl.when(pl.program_id(1) == 0)
    def _():
        rs_ref[...] = _row_scale(ssq_ref, xb_ref.shape[1])

    acc = jnp.dot(xb_ref[...], w_ref[...], preferred_element_type=F32)
    o_ref[...] = (acc * rs_ref[...]).astype(o_ref.dtype)


def matmul_normed(xb, ssq, w, out_dtype, tm, tn, name):
    m, k = xb.shape
    n = w.shape[1]
    tm, tn = _tile(m, tm), _tile(n, tn)
    return pl.pallas_call(
        _mm_normed_kernel,
        grid=(m // tm, n // tn),
        in_specs=[
            pl.BlockSpec((tm, k), lambda i, j: (i, 0)),
            pl.BlockSpec((tm, LANES), lambda i, j: (i, 0)),
            pl.BlockSpec((k, tn), lambda i, j: (0, j)),
        ],
        out_specs=pl.BlockSpec((tm, tn), lambda i, j: (i, j)),
        out_shape=jax.ShapeDtypeStruct((m, n), out_dtype),
        scratch_shapes=[pltpu.VMEM((tm, 1), F32)],
        compiler_params=_cparams("parallel", "arbitrary"),
        name=name,
    )(xb, ssq, w)


def _mm_residual_kernel(a_ref, w_ref, x_ref, o_ref, *norm_refs, scale):
    acc = jnp.dot(a_ref[...], w_ref[...], preferred_element_type=F32)
    y = x_ref[...] + (acc if scale == 1.0 else scale * acc)
    o_ref[...] = y
    if norm_refs:
        yb_ref, ssq_ref = norm_refs
        yb_ref[...] = y.astype(yb_ref.dtype)

        @pl.when(pl.program_id(1) == 0)
        def _():
            ssq_ref[...] = jnp.zeros_like(ssq_ref)

        ssq_ref[...] += _lane_partial_sumsq(y)


def _col_tiles(w, tn):
    k, n = w.shape
    tn = _tile(n, tn)
    return w.reshape(k, n // tn, tn).transpose(1, 0, 2)


def matmul_residual(a, w_tiles, x, scale, tm, name, emit_norm_inputs=False):
    m, k = a.shape
    nj, _, tn = w_tiles.shape
    n = nj * tn
    tm = _tile(m, tm)
    tile = pl.BlockSpec((tm, tn), lambda i, j: (i, j))
    out_specs, out_shape = [tile], [jax.ShapeDtypeStruct((m, n), F32)]
    if emit_norm_inputs:
        out_specs += [tile, pl.BlockSpec((tm, LANES), lambda i, j: (i, 0))]
        out_shape += [jax.ShapeDtypeStruct((m, n), BF16), jax.ShapeDtypeStruct((m, LANES), F32)]
    out = pl.pallas_call(
        functools.partial(_mm_residual_kernel, scale=scale),
        grid=(m // tm, nj),
        in_specs=[
            pl.BlockSpec((tm, k), lambda i, j: (i, 0)),
            pl.BlockSpec((None, k, tn), lambda i, j: (j, 0, 0)),
            tile,
        ],
        out_specs=out_specs,
        out_shape=out_shape,
        compiler_params=_cparams("parallel", "arbitrary"),
        name=name,
    )(a, w_tiles, x)
    return out if emit_norm_inputs else out[0]


def _ffn_up_kernel(xb_ref, ssq_ref, wg_ref, wu_ref, o_ref, rs_ref, *, row_splits):
    @pl.when(pl.program_id(1) == 0)
    def _():
        rs_ref[...] = _row_scale(ssq_ref, xb_ref.shape[1])

    rows = xb_ref.shape[0] // row_splits
    for r in range(row_splits):
        sl = slice(r * rows, (r + 1) * rows)
        xb = xb_ref[sl, :]
        rs = rs_ref[sl, :]
        g = jnp.dot(xb, wg_ref[...], preferred_element_type=F32) * rs
        u = jnp.dot(xb, wu_ref[...], preferred_element_type=F32) * rs
        o_ref[sl, :] = ((g * _sigmoid(g)) * u).astype(o_ref.dtype)


def ffn_up(xb, ssq, wg, wu):
    m, k = xb.shape
    n = wg.shape[1]
    tm, tn = _tile(m, 2048), _tile(n, 256)
    return pl.pallas_call(
        functools.partial(_ffn_up_kernel, row_splits=4 if tm % 1024 == 0 else 1),
        grid=(m // tm, n // tn),
        in_specs=[
            pl.BlockSpec((tm, k), lambda i, j: (i, 0)),
            pl.BlockSpec((tm, LANES), lambda i, j: (i, 0)),
            pl.BlockSpec((k, tn), lambda i, j: (0, j)),
            pl.BlockSpec((k, tn), lambda i, j: (0, j)),
        ],
        out_specs=pl.BlockSpec((tm, tn), lambda i, j: (i, j)),
        out_shape=jax.ShapeDtypeStruct((m, n), BF16),
        scratch_shapes=[pltpu.VMEM((tm, 1), F32)],
        compiler_params=_cparams("parallel", "arbitrary"),
        name="ffn_up",
    )(xb, ssq, wg, wu)


def _gmlp_kernel(u_ref, v_ref, vg_ref, ws_ref, bs_ref, o_ref, vn_ref, *, chunk, groups, gdim):
    v = _gelu_tanh(v_ref[...])
    vn_ref[...] = ((v * _rms_scale(v)) * vg_ref[...]).astype(vn_ref.dtype)
    tm = u_ref.shape[0]
    for c in range(tm // chunk):
        rows = slice(c * chunk, (c + 1) * chunk)
        for g in range(groups):
            cols = slice(g * gdim, (g + 1) * gdim)
            mixed = jnp.dot(ws_ref[g], vn_ref[rows, cols], preferred_element_type=F32) + bs_ref[g]
            o_ref[rows, cols] = (_gelu_tanh(u_ref[rows, cols]) * mixed).astype(o_ref.dtype)


def gmlp_gate(uv, v_norm, ws, bs_b):
    m = uv.shape[0]
    groups, chunk, _ = ws.shape
    gdim = bs_b.shape[2]
    w = groups * gdim
    tm = _tile(m, 2 * chunk)
    return pl.pallas_call(
        functools.partial(_gmlp_kernel, chunk=chunk, groups=groups, gdim=gdim),
        grid=(m // tm,),
        in_specs=[
            pl.BlockSpec((tm, w), lambda i: (i, 0)),
            pl.BlockSpec((tm, w), lambda i: (i, 1)),
            pl.BlockSpec((1, w), lambda i: (0, 0)),
            pl.BlockSpec((groups, chunk, chunk), lambda i: (0, 0, 0)),
            pl.BlockSpec((groups, chunk, gdim), lambda i: (0, 0, 0)),
        ],
        out_specs=pl.BlockSpec((tm, w), lambda i: (i, 0)),
        out_shape=jax.ShapeDtypeStruct((m, w), BF16),
        scratch_shapes=[pltpu.VMEM((tm, w), BF16)],
        compiler_params=_cparams("parallel"),
        name="gmlp_gate",
    )(uv, uv, v_norm.reshape(1, w), ws, bs_b)


def _rope_block(blk, c1, c2):
    return blk * c1 + pltpu.roll(blk, LANES // 2, 1) * c2


def _q_proj_kernel(ql_ref, g_ref, w_ref, c1_ref, c2_ref, o_ref, hn_ref, *, heads_per_tile):
    @pl.when(pl.program_id(1) == 0)
    def _():
        x = ql_ref[...]
        hn_ref[...] = ((x * _rms_scale(x)) * g_ref[...]).astype(hn_ref.dtype)

    acc = jnp.dot(hn_ref[...], w_ref[...], preferred_element_type=F32)
    c1, c2 = c1_ref[...], c2_ref[...]
    for h in range(heads_per_tile):
        lo = h * HEAD_PAD
        o_ref[:, lo:lo + LANES] = (acc[:, lo:lo + LANES] * QK_LOG2_SCALE).astype(o_ref.dtype)
        roped = _rope_block(acc[:, lo + LANES:lo + HEAD_PAD], c1, c2)
        o_ref[:, lo + LANES:lo + HEAD_PAD] = (roped * QK_LOG2_SCALE).astype(o_ref.dtype)


def q_proj(lat, q_norm, w_uq_p, c1, c2, q_rank, seq):
    m = lat.shape[0]
    n = w_uq_p.shape[1]
    tm = _tile(seq, 1024)
    tn = n
    pos_blocks = seq // tm
    return pl.pallas_call(
        functools.partial(_q_proj_kernel, heads_per_tile=tn // HEAD_PAD),
        grid=(m // tm, n // tn),
        in_specs=[
            pl.BlockSpec((tm, q_rank), lambda i, j: (i, 0)),
            pl.BlockSpec((1, q_rank), lambda i, j: (0, 0)),
            pl.BlockSpec((q_rank, tn), lambda i, j: (0, j)),
            pl.BlockSpec((tm, LANES), lambda i, j: (i % pos_blocks, 0)),
            pl.BlockSpec((tm, LANES), lambda i, j: (i % pos_blocks, 0)),
        ],
        out_specs=pl.BlockSpec((tm, tn), lambda i, j: (i, j)),
        out_shape=jax.ShapeDtypeStruct((m, n), BF16),
        scratch_shapes=[pltpu.VMEM((tm, q_rank), BF16)],
        compiler_params=_cparams("parallel", "arbitrary"),
        name="q_proj",
    )(lat, q_norm.reshape(1, q_rank), w_uq_p, c1, c2)


def _kv_proj_kernel(kvl_ref, kr_ref, g_ref, wk_ref, wv_ref, c1_ref, c2_ref, k_ref, v_ref, *, heads):
    x = kvl_ref[...]
    hn = ((x * _rms_scale(x)) * g_ref[...]).astype(BF16)
    kn = jnp.dot(hn, wk_ref[...], preferred_element_type=F32)
    v_ref[...] = jnp.dot(hn, wv_ref[...], preferred_element_type=F32).astype(v_ref.dtype)
    kr = _rope_block(kr_ref[...], c1_ref[...], c2_ref[...]).astype(k_ref.dtype)
    for h in range(heads):
        lo = h * HEAD_PAD
        k_ref[:, lo:lo + LANES] = kn[:, h * QK_NOPE_DIM:(h + 1) * QK_NOPE_DIM].astype(k_ref.dtype)
        k_ref[:, lo + LANES:lo + HEAD_PAD] = kr


def kv_proj(lat, kv_norm, w_k, w_v, c1, c2, q_rank, kv_rank, heads, seq):
    m = lat.shape[0]
    tm = _tile(seq, 512)
    pos_blocks = seq // tm
    kv_blk = q_rank // kv_rank
    kr_blk = (q_rank + kv_rank) // LANES
    return pl.pallas_call(
        functools.partial(_kv_proj_kernel, heads=heads),
        grid=(m // tm,),
        in_specs=[
            pl.BlockSpec((tm, kv_rank), lambda i: (i, kv_blk)),
            pl.BlockSpec((tm, LANES), lambda i: (i, kr_blk)),
            pl.BlockSpec((1, kv_rank), lambda i: (0, 0)),
            pl.BlockSpec(w_k.shape, lambda i: (0, 0)),
            pl.BlockSpec(w_v.shape, lambda i: (0, 0)),
            pl.BlockSpec((tm, LANES), lambda i: (i % pos_blocks, 0)),
            pl.BlockSpec((tm, LANES), lambda i: (i % pos_blocks, 0)),
        ],
        out_specs=[
            pl.BlockSpec((tm, heads * HEAD_PAD), lambda i: (i, 0)),
            pl.BlockSpec((tm, heads * V_HEAD_DIM), lambda i: (i, 0)),
        ],
        out_shape=[
            jax.ShapeDtypeStruct((m, heads * HEAD_PAD), BF16),
            jax.ShapeDtypeStruct((m, heads * V_HEAD_DIM), BF16),
        ],
        compiler_params=_cparams("parallel"),
        name="kv_proj",
    )(lat, lat, kv_norm.reshape(1, kv_rank), w_k, w_v, c1, c2)


def _attn_kernel(q_ref, k_ref, v_ref, o_ref, *, tq, tk):
    seq = k_ref.shape[0]
    nq, nk = seq // tq, seq // tk
    kv_unroll = min(nk, ATTN_UNROLL_CHUNKS)
    q_unroll = max(1, min(nq, ATTN_UNROLL_CHUNKS // kv_unroll))

    def q_body(qi, _):
        q0 = pl.multiple_of(qi * tq, tq)
        q = q_ref[pl.ds(q0, tq), :]

        def kv_body(ki, carry):
            m, l, acc = carry
            k0 = pl.multiple_of(ki * tk, tk)
            k = k_ref[pl.ds(k0, tk), :]
            s = lax.dot_general(q, k, (((1,), (1,)), ((), ())), preferred_element_type=F32)
            m_new = jnp.maximum(m, jnp.max(s, axis=-1, keepdims=True))
            p = jnp.exp2(s - m_new)
            alpha = jnp.exp2(m - m_new)
            l = alpha * l + jnp.sum(p, axis=-1, keepdims=True)
            pv = jnp.dot(p.astype(BF16), v_ref[pl.ds(k0, tk), :], preferred_element_type=F32)
            return m_new, l, alpha * acc + pv

        init = (jnp.full((tq, 1), -jnp.inf, F32), jnp.zeros((tq, 1), F32), jnp.zeros((tq, V_HEAD_DIM), F32))
        _, l, acc = lax.fori_loop(0, nk, kv_body, init, unroll=kv_unroll)
        o_ref[pl.ds(q0, tq), :] = (acc / l).astype(o_ref.dtype)
        return 0

    lax.fori_loop(0, nq, q_body, 0, unroll=q_unroll)


def attention(q, k, v, batch, seq, heads):
    m = q.shape[0]
    tq, tk = _tile(seq, 1024), _tile(seq, 1024)
    return pl.pallas_call(
        functools.partial(_attn_kernel, tq=tq, tk=tk),
        grid=(batch, heads),
        in_specs=[
            pl.BlockSpec((seq, HEAD_PAD), lambda b, h: (b, h)),
            pl.BlockSpec((seq, HEAD_PAD), lambda b, h: (b, h)),
            pl.BlockSpec((seq, V_HEAD_DIM), lambda b, h: (b, h)),
        ],
        out_specs=pl.BlockSpec((seq, V_HEAD_DIM), lambda b, h: (b, h)),
        out_shape=jax.ShapeDtypeStruct((m, heads * V_HEAD_DIM), BF16),
        compiler_params=_cparams("parallel", "parallel"),
        name="attention",
    )(q, k, v)


def _merge_kernel(a_ref, b_ref, wa_ref, wb_ref, ga_ref, gb_ref, o_ref):
    ya = jnp.dot(a_ref[...], wa_ref[...], preferred_element_type=F32)
    yb = jnp.dot(b_ref[...], wb_ref[...], preferred_element_type=F32)
    o_ref[...] = (_sigmoid(ga_ref[...]) * ya + _sigmoid(gb_ref[...]) * yb).astype(o_ref.dtype)


def gated_merge(a, b, wa, wb, gates):
    m, ka = a.shape
    kb = b.shape[1]
    n = wa.shape[1]
    tm, tn = _tile(m, 1024), _tile(n, 512)
    nj = n // tn
    return pl.pallas_call(
        _merge_kernel,
        grid=(m // tm, nj),
        in_specs=[
            pl.BlockSpec((tm, ka), lambda i, j: (i, 0)),
            pl.BlockSpec((tm, kb), lambda i, j: (i, 0)),
            pl.BlockSpec((ka, tn), lambda i, j: (0, j)),
            pl.BlockSpec((kb, tn), lambda i, j: (0, j)),
            pl.BlockSpec((tm, tn), lambda i, j: (i, j)),
            pl.BlockSpec((tm, tn), lambda i, j: (i, j + nj)),
        ],
        out_specs=pl.BlockSpec((tm, tn), lambda i, j: (i, j)),
        out_shape=jax.ShapeDtypeStruct((m, n), BF16),
        compiler_params=_cparams("parallel", "parallel"),
        name="gated_merge",
    )(a, b, wa, wb, gates, gates)


def _ple_final_kernel(x_ref, xb_ref, ssq_ref, wg_ref, p_ref, wp_ref, gp_ref, gf_ref, o_ref, rs_ref, t_ref, *, nj, tn):
    j = pl.program_id(1)

    @pl.when(j == 0)
    def _():
        rs_ref[...] = _row_scale(ssq_ref, xb_ref.shape[1])

    g = jnp.dot(xb_ref[...], wg_ref[...], preferred_element_type=F32) * rs_ref[...]
    pp = jnp.dot(p_ref[...].astype(BF16), wp_ref[...], preferred_element_type=F32)
    t_ref[j] = _sigmoid(g) * pp

    @pl.when(j == nj - 1)
    def _():
        d = nj * tn
        cols = [slice(c * tn, (c + 1) * tn) for c in range(nj)]
        ssq = sum(jnp.sum(t_ref[c] * t_ref[c], axis=-1, keepdims=True) for c in range(nj))
        t_scale = lax.rsqrt(ssq / d + RMS_EPS)
        ssq = jnp.zeros_like(ssq)
        for c in range(nj):
            xc = x_ref[:, cols[c]] + (t_ref[c] * t_scale) * gp_ref[:, cols[c]]
            t_ref[c] = xc
            ssq = ssq + jnp.sum(xc * xc, axis=-1, keepdims=True)
        x_scale = lax.rsqrt(ssq / d + RMS_EPS)
        for c in range(nj):
            o_ref[:, cols[c]] = (t_ref[c] * x_scale) * gf_ref[:, cols[c]]


def ple_final(x, xb, ssq, wg_tiles, p, wp, post_g, final_g):
    m, d = x.shape
    kp = p.shape[1]
    nj, _, tn = wg_tiles.shape
    tm = _tile(m, 512)
    row = pl.BlockSpec((tm, d), lambda i, j: (i, 0))
    vec = pl.BlockSpec((1, d), lambda i, j: (0, 0))
    return pl.pallas_call(
        functools.partial(_ple_final_kernel, nj=nj, tn=tn),
        grid=(m // tm, nj),
        in_specs=[
            pl.BlockSpec((tm, d), lambda i, j: (i, 0), pipeline_mode=pl.Buffered(1)),
            row,
            pl.BlockSpec((tm, LANES), lambda i, j: (i, 0)),
            pl.BlockSpec((None, d, tn), lambda i, j: (j, 0, 0)),
            pl.BlockSpec((tm, kp), lambda i, j: (i, 0)),
            pl.BlockSpec((kp, tn), lambda i, j: (0, j)),
            vec,
            vec,
        ],
        out_specs=row,
        out_shape=jax.ShapeDtypeStruct((m, d), F32),
        scratch_shapes=[pltpu.VMEM((tm, 1), F32), pltpu.VMEM((nj, tm, tn), F32)],
        compiler_params=_cparams("parallel", "arbitrary"),
        name="ple_final",
    )(x, xb, ssq, wg_tiles, p, wp, post_g.reshape(1, d), final_g.reshape(1, d))


def _rope_tables(seq):
    inv = 1.0 / (ROPE_THETA ** (jnp.arange(0, QK_ROPE_DIM, 2, dtype=F32) / QK_ROPE_DIM))
    ang = jnp.arange(seq, dtype=F32)[:, None] * inv[None, :]
    cos, sin = jnp.cos(ang), jnp.sin(ang)
    zero = jnp.zeros((seq, LANES - QK_ROPE_DIM), F32)
    return jnp.concatenate([cos, cos, zero], axis=1), jnp.concatenate([-sin, sin, zero], axis=1)


def _rotate_half_cols(w):
    half = QK_ROPE_DIM // 2
    return jnp.concatenate([w[..., half:], w[..., :half]], axis=-1)


def _prepare_weights(ffn1_norm, ffn1_wg, ffn1_wu, ffn1_wd, mix_norm, w_in, gm_ws, gm_bs, w_out_a, w_uq, w_ukv,
                     w_out_b, w_out, ffn2_norm, ffn2_wg, ffn2_wu, ffn2_wd, ple_gate_norm, w_ple_gate, w_ple_proj,
                     q_rank, kv_rank, gm_width):
    def up(w, gain):
        return (gain[:, None] * w).astype(BF16)

    def down(w):
        return _col_tiles(w.astype(BF16), COL_TILE)

    w_in = mix_norm[:, None] * w_in
    o_q = 2 * gm_width
    o_kv = o_q + q_rank
    o_kr = o_kv + kv_rank
    o_gate = o_kr + QK_ROPE_DIM
    w_kr = w_in[:, o_kr:o_gate]
    w_lat = jnp.concatenate([w_in[:, o_q:o_kr], w_kr, _rotate_half_cols(w_kr)], axis=1)

    heads = w_uq.shape[1] // (QK_NOPE_DIM + QK_ROPE_DIM)
    uq = w_uq.reshape(q_rank, heads, QK_NOPE_DIM + QK_ROPE_DIM)
    uq_rope = uq[:, :, QK_NOPE_DIM:]
    w_uq_p = jnp.concatenate([uq[:, :, :QK_NOPE_DIM], uq_rope, _rotate_half_cols(uq_rope)], axis=2)
    ukv = w_ukv.reshape(kv_rank, heads, QK_NOPE_DIM + V_HEAD_DIM)

    gdim = gm_width // gm_ws.shape[0]
    return dict(
        ffn1=(up(ffn1_wg, ffn1_norm), up(ffn1_wu, ffn1_norm), down(ffn1_wd)),
        ffn2=(up(ffn2_wg, ffn2_norm), up(ffn2_wu, ffn2_norm), down(ffn2_wd)),
        w_uv=w_in[:, :o_q].astype(BF16),
        w_lat=w_lat.astype(BF16),
        w_gates=w_in[:, o_gate:].astype(BF16),
        ws=gm_ws.astype(BF16),
        bs_b=jnp.broadcast_to(gm_bs[:, :, None], gm_bs.shape + (gdim,)).astype(F32),
        w_out_a=w_out_a.astype(BF16),
        w_uq_p=w_uq_p.reshape(q_rank, heads * HEAD_PAD).astype(BF16),
        w_k=ukv[:, :, :QK_NOPE_DIM].reshape(kv_rank, heads * QK_NOPE_DIM).astype(BF16),
        w_v=ukv[:, :, QK_NOPE_DIM:].reshape(kv_rank, heads * V_HEAD_DIM).astype(BF16),
        w_out_b=w_out_b.astype(BF16),
        w_out=down(w_out),
        w_ple_gate=_col_tiles(up(w_ple_gate, ple_gate_norm), COL_TILE),
        w_ple_proj=w_ple_proj.astype(BF16),
        heads=heads,
    )


def _trunk(x3d, p3d, w, norms, q_rank, kv_rank):
    batch, seq, d = x3d.shape
    x = x3d.reshape(batch * seq, d)
    p = p3d.reshape(batch * seq, p3d.shape[-1])
    heads = w["heads"]

    wg, wu, wd = w["ffn1"]
    xb, ssq = cast_sumsq(x)
    x, xb, ssq = matmul_residual(ffn_up(xb, ssq, wg, wu), wd, x, 0.5, 512, "ffn_down", emit_norm_inputs=True)

    uv = matmul_normed(xb, ssq, w["w_uv"], F32, 1024, 512, "in_proj_uv")
    lat = matmul_normed(xb, ssq, w["w_lat"], F32, 512, w["w_lat"].shape[1], "in_proj_lat")
    gates = matmul_normed(xb, ssq, w["w_gates"], F32, 1024, 512, "in_proj_gates")

    gm = gmlp_gate(uv, norms["gm_v"], w["ws"], w["bs_b"])

    c1, c2 = _rope_tables(seq)
    q = q_proj(lat, norms["q"], w["w_uq_p"], c1, c2, q_rank, seq)
    k, v = kv_proj(lat, norms["kv"], w["w_k"], w["w_v"], c1, c2, q_rank, kv_rank, heads, seq)
    o = attention(q, k, v, batch, seq, heads)

    merged = gated_merge(gm, o, w["w_out_a"], w["w_out_b"], gates)
    x, xb, ssq = matmul_residual(merged, w["w_out"], x, 1.0, 1024, "out_proj", emit_norm_inputs=True)

    wg, wu, wd = w["ffn2"]
    x, xb, ssq = matmul_residual(ffn_up(xb, ssq, wg, wu), wd, x, 0.5, 512, "ffn_down", emit_norm_inputs=True)

    y = ple_final(x, xb, ssq, w["w_ple_gate"], p, w["w_ple_proj"], norms["ple_post"], norms["final"])
    return y.reshape(batch, seq, d)


def kernel(x_prompt, x_sample, p_prompt, p_sample, ffn1_norm, ffn1_wg, ffn1_wu, ffn1_wd, mix_norm, w_in, gm_v_norm, gm_ws, gm_bs, w_out_a, q_norm, w_uq, kv_norm, w_ukv, w_out_b, w_out, ffn2_norm, ffn2_wg, ffn2_wu, ffn2_wd, ple_gate_norm, w_ple_gate, w_ple_proj, ple_post_norm, final_norm):
    depth = ffn1_norm.shape[0]
    q_rank = q_norm.shape[1]
    kv_rank = kv_norm.shape[1]
    gm_width = gm_v_norm.shape[1]
    assert q_rank % kv_rank == 0 and (q_rank + kv_rank) % LANES == 0

    xs = [x_prompt, x_sample]
    for l in range(depth):
        w = _prepare_weights(ffn1_norm[l], ffn1_wg[l], ffn1_wu[l], ffn1_wd[l], mix_norm[l], w_in[l], gm_ws[l],
                             gm_bs[l], w_out_a[l], w_uq[l], w_ukv[l], w_out_b[l], w_out[l], ffn2_norm[l], ffn2_wg[l],
                             ffn2_wu[l], ffn2_wd[l], ple_gate_norm[l], w_ple_gate[l], w_ple_proj[l], q_rank, kv_rank,
                             gm_width)
        norms = dict(gm_v=gm_v_norm[l], q=q_norm[l], kv=kv_norm[l], ple_post=ple_post_norm[l], final=final_norm)
        assert depth == 1
        xs = [_trunk(x, p[l], w, norms, q_rank, kv_rank) for x, p in zip(xs, [p_prompt, p_sample])]
    return tuple(xs)
```

```python
import functools

import jax
import jax.numpy as jnp
import numpy as np
from jax import lax
from jax.experimental import pallas as pl
from jax.experimental.pallas import tpu as pltpu

F32 = jnp.float32
BF16 = jnp.bfloat16

RMS_EPS = 1e-6
ROPE_THETA = 10000.0
QK_NOPE_DIM = 128
QK_ROPE_DIM = 64
V_HEAD_DIM = 128
LANES = 128
HEAD_PAD = 2 * LANES
QK_LOG2_SCALE = float((QK_NOPE_DIM + QK_ROPE_DIM) ** -0.5 * np.log2(np.e))
ATTN_UNROLL_CHUNKS = 8
VMEM_LIMIT = 56 * 1024 * 1024


def _cparams(*sem):
    return pltpu.CompilerParams(dimension_semantics=sem, vmem_limit_bytes=VMEM_LIMIT)


def _tile(n, want):
    if n <= want:
        return n
    t = want
    while n % t:
        t //= 2
    return t


def _sigmoid(x):
    return 1.0 / (1.0 + jnp.exp(-x))


def _gelu_tanh(x):
    c = np.float32(np.sqrt(2.0 / np.pi))
    return x * (0.5 * (1.0 + jnp.tanh(c * (x + 0.044715 * (x * x * x)))))


def _rms_scale(x):
    return lax.rsqrt(jnp.mean(x * x, axis=-1, keepdims=True) + RMS_EPS)


def _lane_partial_sumsq(x):
    x2 = x * x
    return sum(x2[:, c * LANES:(c + 1) * LANES] for c in range(x.shape[1] // LANES))


def _row_scale(ssq_ref, width):
    return lax.rsqrt(jnp.sum(ssq_ref[...], axis=-1, keepdims=True) / width + RMS_EPS)


def _cast_sumsq_kernel(x_ref, xb_ref, ssq_ref):
    x = x_ref[...]
    xb_ref[...] = x.astype(xb_ref.dtype)
    ssq_ref[...] = _lane_partial_sumsq(x)


def cast_sumsq(x):
    m, d = x.shape
    tm = _tile(m, 256)
    return pl.pallas_call(
        _cast_sumsq_kernel,
        grid=(m // tm,),
        in_specs=[pl.BlockSpec((tm, d), lambda i: (i, 0))],
        out_specs=[pl.BlockSpec((tm, d), lambda i: (i, 0)), pl.BlockSpec((tm, LANES), lambda i: (i, 0))],
        out_shape=[jax.ShapeDtypeStruct((m, d), BF16), jax.ShapeDtypeStruct((m, LANES), F32)],
        compiler_params=_cparams("parallel"),
        name="cast_sumsq",
    )(x)


def _mm_normed_kernel(xb_ref, ssq_ref, w_ref, o_ref, rs_ref):
    @pl.when(pl.program_id(1) == 0)
    def _():
        rs_ref[...] = _row_scale(ssq_ref, xb_ref.shape[1])

    acc = jnp.dot(xb_ref[...], w_ref[...], preferred_element_type=F32)
    o_ref[...] = (acc * rs_ref[...]).astype(o_ref.dtype)


def matmul_normed(xb, ssq, w, out_dtype, tm, tn, name):
    m, k = xb.shape
    n = w.shape[1]
    tm, tn = _tile(m, tm), _tile(n, tn)
    return pl.pallas_call(
        _mm_normed_kernel,
        grid=(m // tm, n // tn),
        in_specs=[
            pl.BlockSpec((tm, k), lambda i, j: (i, 0)),
            pl.BlockSpec((tm, LANES), lambda i, j: (i, 0)),
            pl.BlockSpec((k, tn), lambda i, j: (0, j)),
        ],
        out_specs=pl.BlockSpec((tm, tn), lambda i, j: (i, j)),
        out_shape=jax.ShapeDtypeStruct((m, n), out_dtype),
        scratch_shapes=[pltpu.VMEM((tm, 1), F32)],
        compiler_params=_cparams("parallel", "arbitrary"),
        name=name,
    )(xb, ssq, w)


def _mm_residual_kernel(a_ref, w_ref, x_ref, o_ref, *norm_refs, scale):
    acc = jnp.dot(a_ref[...], w_ref[...], preferred_element_type=F32)
    y = x_ref[...] + (acc if scale == 1.0 else scale * acc)
    o_ref[...] = y
    if norm_refs:
        yb_ref, ssq_ref = norm_refs
        yb_ref[...] = y.astype(yb_ref.dtype)

        @pl.when(pl.program_id(1) == 0)
        def _():
            ssq_ref[...] = jnp.zeros_like(ssq_ref)

        ssq_ref[...] += _lane_partial_sumsq(y)


def matmul_residual(a, w, x, scale, tm, tn, name, emit_norm_inputs=False):
    m, k = a.shape
    n = w.shape[1]
    tm, tn = _tile(m, tm), _tile(n, tn)
    tile = pl.BlockSpec((tm, tn), lambda i, j: (i, j))
    out_specs, out_shape = [tile], [jax.ShapeDtypeStruct((m, n), F32)]
    if emit_norm_inputs:
        out_specs += [tile, pl.BlockSpec((tm, LANES), lambda i, j: (i, 0))]
        out_shape += [jax.ShapeDtypeStruct((m, n), BF16), jax.ShapeDtypeStruct((m, LANES), F32)]
    out = pl.pallas_call(
        functools.partial(_mm_residual_kernel, scale=scale),
        grid=(m // tm, n // tn),
        in_specs=[pl.BlockSpec((tm, k), lambda i, j: (i, 0)), pl.BlockSpec((k, tn), lambda i, j: (0, j)), tile],
        out_specs=out_specs,
        out_shape=out_shape,
        compiler_params=_cparams("parallel", "arbitrary"),
        name=name,
    )(a, w, x)
    return out if emit_norm_inputs else out[0]


def _ffn_up_kernel(xb_ref, ssq_ref, wg_ref, wu_ref, *refs, row_splits, cast_has_gain):
    n_casts = len(cast_has_gain)
    n_cast_in = n_casts + sum(cast_has_gain)
    cast_in, o_ref, cast_out, rs_ref = refs[:n_cast_in], refs[n_cast_in], refs[n_cast_in + 1:-1], refs[-1]

    @pl.when(pl.program_id(1) == 0)
    def _():
        rs_ref[...] = _row_scale(ssq_ref, xb_ref.shape[1])

    rows = xb_ref.shape[0] // row_splits
    for r in range(row_splits):
        sl = slice(r * rows, (r + 1) * rows)
        xb = xb_ref[sl, :]
        rs = rs_ref[sl, :]
        g = jnp.dot(xb, wg_ref[...], preferred_element_type=F32) * rs
        u = jnp.dot(xb, wu_ref[...], preferred_element_type=F32) * rs
        o_ref[sl, :] = ((g * _sigmoid(g)) * u).astype(o_ref.dtype)

    pos = 0
    for c, has_gain in enumerate(cast_has_gain):
        w = cast_in[pos][...]
        if has_gain:
            w = cast_in[pos + 1][...] * w
        cast_out[c][...] = w.astype(cast_out[c].dtype)
        pos += 1 + has_gain


def ffn_up(xb, ssq, wg, wu, weight_casts=()):
    m, k = xb.shape
    n = wg.shape[1]
    tm, tn = _tile(m, 2048), _tile(n, 256)
    ni, nj = m // tm, n // tn
    in_specs = [
        pl.BlockSpec((tm, k), lambda i, j: (i, 0)),
        pl.BlockSpec((tm, LANES), lambda i, j: (i, 0)),
        pl.BlockSpec((k, tn), lambda i, j: (0, j)),
        pl.BlockSpec((k, tn), lambda i, j: (0, j)),
    ]
    out_specs = [pl.BlockSpec((tm, tn), lambda i, j: (i, j))]
    out_shape = [jax.ShapeDtypeStruct((m, n), BF16)]
    cast_args = []
    for w, gain in weight_casts:
        r, c = w.shape
        if r % ni == 0 and c % nj == 0:
            blk, idx, gidx = (r // ni, c // nj), (lambda i, j: (i, j)), (lambda i, j: (i, 0))
        else:
            blk, idx, gidx = (r // nj, c // ni), (lambda i, j: (j, i)), (lambda i, j: (j, 0))
        assert blk[0] * (r // blk[0]) == r and blk[1] * (c // blk[1]) == c and (r // blk[0]) * (c // blk[1]) == ni * nj
        in_specs.append(pl.BlockSpec(blk, idx))
        cast_args.append(w)
        if gain is not None:
            in_specs.append(pl.BlockSpec((blk[0], 1), gidx))
            cast_args.append(gain.reshape(r, 1))
        out_specs.append(pl.BlockSpec(blk, idx))
        out_shape.append(jax.ShapeDtypeStruct((r, c), BF16))
    out = pl.pallas_call(
        functools.partial(_ffn_up_kernel, row_splits=4 if tm % 1024 == 0 else 1,
                          cast_has_gain=tuple(g is not None for _, g in weight_casts)),
        grid=(ni, nj),
        in_specs=in_specs,
        out_specs=out_specs,
        out_shape=out_shape,
        scratch_shapes=[pltpu.VMEM((tm, 1), F32)],
        compiler_params=_cparams("parallel", "arbitrary"),
        name="ffn_up",
    )(xb, ssq, wg, wu, *cast_args)
    return out[0], out[1:]


def _gmlp_kernel(u_ref, v_ref, vg_ref, ws_ref, bs_ref, o_ref, vn_ref, *, chunk, groups, gdim):
    v = _gelu_tanh(v_ref[...])
    vn_ref[...] = ((v * _rms_scale(v)) * vg_ref[...]).astype(vn_ref.dtype)
    tm = u_ref.shape[0]
    for c in range(tm // chunk):
        rows = slice(c * chunk, (c + 1) * chunk)
        for g in range(groups):
            cols = slice(g * gdim, (g + 1) * gdim)
            mixed = jnp.dot(ws_ref[g], vn_ref[rows, cols], preferred_element_type=F32) + bs_ref[g]
            o_ref[rows, cols] = (_gelu_tanh(u_ref[rows, cols]) * mixed).astype(o_ref.dtype)


def gmlp_gate(uv, v_norm, ws, bs_b):
    m = uv.shape[0]
    groups, chunk, _ = ws.shape
    gdim = bs_b.shape[2]
    w = groups * gdim
    tm = _tile(m, 2 * chunk)
    return pl.pallas_call(
        functools.partial(_gmlp_kernel, chunk=chunk, groups=groups, gdim=gdim),
        grid=(m // tm,),
        in_specs=[
            pl.BlockSpec((tm, w), lambda i: (i, 0)),
            pl.BlockSpec((tm, w), lambda i: (i, 1)),
            pl.BlockSpec((1, w), lambda i: (0, 0)),
            pl.BlockSpec((groups, chunk, chunk), lambda i: (0, 0, 0)),
            pl.BlockSpec((groups, chunk, gdim), lambda i: (0, 0, 0)),
        ],
        out_specs=pl.BlockSpec((tm, w), lambda i: (i, 0)),
        out_shape=jax.ShapeDtypeStruct((m, w), BF16),
        scratch_shapes=[pltpu.VMEM((tm, w), BF16)],
        compiler_params=_cparams("parallel"),
        name="gmlp_gate",
    )(uv, uv, v_norm.reshape(1, w), ws, bs_b)


def _rope_block(blk, c1, c2):
    return blk * c1 + pltpu.roll(blk, LANES // 2, 1) * c2


def _q_proj_kernel(ql_ref, g_ref, w_ref, c1_ref, c2_ref, o_ref, hn_ref, *, heads_per_tile):
    @pl.when(pl.program_id(1) == 0)
    def _():
        x = ql_ref[...]
        hn_ref[...] = ((x * _rms_scale(x)) * g_ref[...]).astype(hn_ref.dtype)

    acc = jnp.dot(hn_ref[...], w_ref[...], preferred_element_type=F32)
    c1, c2 = c1_ref[...], c2_ref[...]
    for h in range(heads_per_tile):
        lo = h * HEAD_PAD
        o_ref[:, lo:lo + LANES] = (acc[:, lo:lo + LANES] * QK_LOG2_SCALE).astype(o_ref.dtype)
        roped = _rope_block(acc[:, lo + LANES:lo + HEAD_PAD], c1, c2)
        o_ref[:, lo + LANES:lo + HEAD_PAD] = (roped * QK_LOG2_SCALE).astype(o_ref.dtype)


def q_proj(lat, q_norm, w_uq_p, c1, c2, q_rank, seq):
    m = lat.shape[0]
    n = w_uq_p.shape[1]
    tm = _tile(seq, 1024)
    tn = n
    pos_blocks = seq // tm
    return pl.pallas_call(
        functools.partial(_q_proj_kernel, heads_per_tile=tn // HEAD_PAD),
        grid=(m // tm, n // tn),
        in_specs=[
            pl.BlockSpec((tm, q_rank), lambda i, j: (i, 0)),
            pl.BlockSpec((1, q_rank), lambda i, j: (0, 0)),
            pl.BlockSpec((q_rank, tn), lambda i, j: (0, j)),
            pl.BlockSpec((tm, LANES), lambda i, j: (i % pos_blocks, 0)),
            pl.BlockSpec((tm, LANES), lambda i, j: (i % pos_blocks, 0)),
        ],
        out_specs=pl.BlockSpec((tm, tn), lambda i, j: (i, j)),
        out_shape=jax.ShapeDtypeStruct((m, n), BF16),
        scratch_shapes=[pltpu.VMEM((tm, q_rank), BF16)],
        compiler_params=_cparams("parallel", "arbitrary"),
        name="q_proj",
    )(lat, q_norm.reshape(1, q_rank), w_uq_p, c1, c2)


def _kv_proj_kernel(kvl_ref, kr_ref, g_ref, wk_ref, wv_ref, c1_ref, c2_ref, k_ref, v_ref, *, heads):
    x = kvl_ref[...]
    hn = ((x * _rms_scale(x)) * g_ref[...]).astype(BF16)
    kn = jnp.dot(hn, wk_ref[...], preferred_element_type=F32)
    v_ref[...] = jnp.dot(hn, wv_ref[...], preferred_element_type=F32).astype(v_ref.dtype)
    kr = _rope_block(kr_ref[...], c1_ref[...], c2_ref[...]).astype(k_ref.dtype)
    for h in range(heads):
        lo = h * HEAD_PAD
        k_ref[:, lo:lo + LANES] = kn[:, h * QK_NOPE_DIM:(h + 1) * QK_NOPE_DIM].astype(k_ref.dtype)
        k_ref[:, lo + LANES:lo + HEAD_PAD] = kr


def kv_proj(lat, kv_norm, w_k, w_v, c1, c2, q_rank, kv_rank, heads, seq):
    m = lat.shape[0]
    tm = _tile(seq, 512)
    pos_blocks = seq // tm
    kv_blk = q_rank // kv_rank
    kr_blk = (q_rank + kv_rank) // LANES
    return pl.pallas_call(
        functools.partial(_kv_proj_kernel, heads=heads),
        grid=(m // tm,),
        in_specs=[
            pl.BlockSpec((tm, kv_rank), lambda i: (i, kv_blk)),
            pl.BlockSpec((tm, LANES), lambda i: (i, kr_blk)),
            pl.BlockSpec((1, kv_rank), lambda i: (0, 0)),
            pl.BlockSpec(w_k.shape, lambda i: (0, 0)),
            pl.BlockSpec(w_v.shape, lambda i: (0, 0)),
            pl.BlockSpec((tm, LANES), lambda i: (i % pos_blocks, 0)),
            pl.BlockSpec((tm, LANES), lambda i: (i % pos_blocks, 0)),
        ],
        out_specs=[
            pl.BlockSpec((tm, heads * HEAD_PAD), lambda i: (i, 0)),
            pl.BlockSpec((tm, heads * V_HEAD_DIM), lambda i: (i, 0)),
        ],
        out_shape=[
            jax.ShapeDtypeStruct((m, heads * HEAD_PAD), BF16),
            jax.ShapeDtypeStruct((m, heads * V_HEAD_DIM), BF16),
        ],
        compiler_params=_cparams("parallel"),
        name="kv_proj",
    )(lat, lat, kv_norm.reshape(1, kv_rank), w_k, w_v, c1, c2)


def _attn_kernel(q_ref, k_ref, v_ref, o_ref, *, tq, tk):
    seq = k_ref.shape[0]
    nq, nk = seq // tq, seq // tk
    kv_unroll = min(nk, ATTN_UNROLL_CHUNKS)
    q_unroll = max(1, min(nq, ATTN_UNROLL_CHUNKS // kv_unroll))

    def q_body(qi, _):
        q0 = pl.multiple_of(qi * tq, tq)
        q = q_ref[pl.ds(q0, tq), :]

        def kv_body(ki, carry):
            m, l, acc = carry
            k0 = pl.multiple_of(ki * tk, tk)
            k = k_ref[pl.ds(k0, tk), :]
            s = lax.dot_general(q, k, (((1,), (1,)), ((), ())), preferred_element_type=F32)
            m_new = jnp.maximum(m, jnp.max(s, axis=-1, keepdims=True))
            p = jnp.exp2(s - m_new)
            alpha = jnp.exp2(m - m_new)
            l = alpha * l + jnp.sum(p, axis=-1, keepdims=True)
            pv = jnp.dot(p.astype(BF16), v_ref[pl.ds(k0, tk), :], preferred_element_type=F32)
            return m_new, l, alpha * acc + pv

        init = (jnp.full((tq, 1), -jnp.inf, F32), jnp.zeros((tq, 1), F32), jnp.zeros((tq, V_HEAD_DIM), F32))
        _, l, acc = lax.fori_loop(0, nk, kv_body, init, unroll=kv_unroll)
        o_ref[pl.ds(q0, tq), :] = (acc / l).astype(o_ref.dtype)
        return 0

    lax.fori_loop(0, nq, q_body, 0, unroll=q_unroll)


def attention(q, k, v, batch, seq, heads):
    m = q.shape[0]
    tq, tk = _tile(seq, 1024), _tile(seq, 1024)
    return pl.pallas_call(
        functools.partial(_attn_kernel, tq=tq, tk=tk),
        grid=(batch, heads),
        in_specs=[
            pl.BlockSpec((seq, HEAD_PAD), lambda b, h: (b, h)),
            pl.BlockSpec((seq, HEAD_PAD), lambda b, h: (b, h)),
            pl.BlockSpec((seq, V_HEAD_DIM), lambda b, h: (b, h)),
        ],
        out_specs=pl.BlockSpec((seq, V_HEAD_DIM), lambda b, h: (b, h)),
        out_shape=jax.ShapeDtypeStruct((m, heads * V_HEAD_DIM), BF16),
        compiler_params=_cparams("parallel", "parallel"),
        name="attention",
    )(q, k, v)


def _merge_kernel(a_ref, b_ref, wa_ref, wb_ref, ga_ref, gb_ref, o_ref):
    ya = jnp.dot(a_ref[...], wa_ref[...], preferred_element_type=F32)
    yb = jnp.dot(b_ref[...], wb_ref[...], preferred_element_type=F32)
    o_ref[...] = (_sigmoid(ga_ref[...]) * ya + _sigmoid(gb_ref[...]) * yb).astype(o_ref.dtype)


def gated_merge(a, b, wa, wb, gates):
    m, ka = a.shape
    kb = b.shape[1]
    n = wa.shape[1]
    tm, tn = _tile(m, 1024), _tile(n, 512)
    nj = n // tn
    return pl.pallas_call(
        _merge_kernel,
        grid=(m // tm, nj),
        in_specs=[
            pl.BlockSpec((tm, ka), lambda i, j: (i, 0)),
            pl.BlockSpec((tm, kb), lambda i, j: (i, 0)),
            pl.BlockSpec((ka, tn), lambda i, j: (0, j)),
            pl.BlockSpec((kb, tn), lambda i, j: (0, j)),
            pl.BlockSpec((tm, tn), lambda i, j: (i, j)),
            pl.BlockSpec((tm, tn), lambda i, j: (i, j + nj)),
        ],
        out_specs=pl.BlockSpec((tm, tn), lambda i, j: (i, j)),
        out_shape=jax.ShapeDtypeStruct((m, n), BF16),
        compiler_params=_cparams("parallel", "parallel"),
        name="gated_merge",
    )(a, b, wa, wb, gates, gates)


def _ple_gate_kernel(xb_ref, ssq_ref, wg_ref, p_ref, wp_ref, o_ref, rs_ref):
    @pl.when(pl.program_id(1) == 0)
    def _():
        rs_ref[...] = _row_scale(ssq_ref, xb_ref.shape[1])

    g = jnp.dot(xb_ref[...], wg_ref[...], preferred_element_type=F32) * rs_ref[...]
    pp = jnp.dot(p_ref[...].astype(BF16), wp_ref[...], preferred_element_type=F32)
    o_ref[...] = _sigmoid(g) * pp


def ple_gate(xb, ssq, wg, p, wp):
    m, k = xb.shape
    kp = p.shape[1]
    n = wg.shape[1]
    tm, tn = _tile(m, 1024), _tile(n, 512)
    return pl.pallas_call(
        _ple_gate_kernel,
        grid=(m // tm, n // tn),
        in_specs=[
            pl.BlockSpec((tm, k), lambda i, j: (i, 0)),
            pl.BlockSpec((tm, LANES), lambda i, j: (i, 0)),
            pl.BlockSpec((k, tn), lambda i, j: (0, j)),
            pl.BlockSpec((tm, kp), lambda i, j: (i, 0)),
            pl.BlockSpec((kp, tn), lambda i, j: (0, j)),
        ],
        out_specs=pl.BlockSpec((tm, tn), lambda i, j: (i, j)),
        out_shape=jax.ShapeDtypeStruct((m, n), F32),
        scratch_shapes=[pltpu.VMEM((tm, 1), F32)],
        compiler_params=_cparams("parallel", "arbitrary"),
        name="ple_gate",
    )(xb, ssq, wg, p, wp)


def _final_kernel(x_ref, t_ref, gp_ref, gf_ref, o_ref):
    t = t_ref[...]
    x = x_ref[...] + (t * _rms_scale(t)) * gp_ref[...]
    o_ref[...] = (x * _rms_scale(x)) * gf_ref[...]


def ple_add_final_norm(x, t, post_g, final_g):
    m, d = x.shape
    tm = _tile(m, 256)
    row = pl.BlockSpec((tm, d), lambda i: (i, 0))
    vec = pl.BlockSpec((1, d), lambda i: (0, 0))
    return pl.pallas_call(
        _final_kernel,
        grid=(m // tm,),
        in_specs=[row, row, vec, vec],
        out_specs=row,
        out_shape=jax.ShapeDtypeStruct((m, d), F32),
        compiler_params=_cparams("parallel"),
        name="ple_add_final_norm",
    )(x, t, post_g.reshape(1, d), final_g.reshape(1, d))


def _rope_tables(seq):
    inv = 1.0 / (ROPE_THETA ** (jnp.arange(0, QK_ROPE_DIM, 2, dtype=F32) / QK_ROPE_DIM))
    ang = jnp.arange(seq, dtype=F32)[:, None] * inv[None, :]
    cos, sin = jnp.cos(ang), jnp.sin(ang)
    zero = jnp.zeros((seq, LANES - QK_ROPE_DIM), F32)
    return jnp.concatenate([cos, cos, zero], axis=1), jnp.concatenate([-sin, sin, zero], axis=1)


def _rotate_half_cols(w):
    half = QK_ROPE_DIM // 2
    return jnp.concatenate([w[..., half:], w[..., :half]], axis=-1)


def _prepare_weights(ffn1_norm, ffn1_wg, ffn1_wu, ffn1_wd, mix_norm, w_in, gm_ws, gm_bs, w_out_a, w_uq, w_ukv,
                     w_out_b, w_out, ffn2_norm, ffn2_wg, ffn2_wu, ffn2_wd, ple_gate_norm, w_ple_gate, w_ple_proj,
                     q_rank, kv_rank, gm_width):
    def up(w, gain):
        return (gain[:, None] * w).astype(BF16)

    w_in = mix_norm[:, None] * w_in
    o_q = 2 * gm_width
    o_kv = o_q + q_rank
    o_kr = o_kv + kv_rank
    o_gate = o_kr + QK_ROPE_DIM
    w_kr = w_in[:, o_kr:o_gate]
    w_lat = jnp.concatenate([w_in[:, o_q:o_kr], w_kr, _rotate_half_cols(w_kr)], axis=1)

    heads = w_uq.shape[1] // (QK_NOPE_DIM + QK_ROPE_DIM)
    uq = w_uq.reshape(q_rank, heads, QK_NOPE_DIM + QK_ROPE_DIM)
    uq_rope = uq[:, :, QK_NOPE_DIM:]
    w_uq_p = jnp.concatenate([uq[:, :, :QK_NOPE_DIM], uq_rope, _rotate_half_cols(uq_rope)], axis=2)
    ukv = w_ukv.reshape(kv_rank, heads, QK_NOPE_DIM + V_HEAD_DIM)

    gdim = gm_width // gm_ws.shape[0]
    return dict(
        ffn1_up=(up(ffn1_wg, ffn1_norm), up(ffn1_wu, ffn1_norm)),
        late_casts=((ffn1_wd, None), (ffn2_wg, ffn2_norm), (ffn2_wu, ffn2_norm), (ffn2_wd, None)),
        w_uv=w_in[:, :o_q].astype(BF16),
        w_lat=w_lat.astype(BF16),
        w_gates=w_in[:, o_gate:].astype(BF16),
        ws=gm_ws.astype(BF16),
        bs_b=jnp.broadcast_to(gm_bs[:, :, None], gm_bs.shape + (gdim,)).astype(F32),
        w_out_a=w_out_a.astype(BF16),
        w_uq_p=w_uq_p.reshape(q_rank, heads * HEAD_PAD).astype(BF16),
        w_k=ukv[:, :, :QK_NOPE_DIM].reshape(kv_rank, heads * QK_NOPE_DIM).astype(BF16),
        w_v=ukv[:, :, QK_NOPE_DIM:].reshape(kv_rank, heads * V_HEAD_DIM).astype(BF16),
        w_out_b=w_out_b.astype(BF16),
        w_out=w_out.astype(BF16),
        w_ple_gate=up(w_ple_gate, ple_gate_norm),
        w_ple_proj=w_ple_proj.astype(BF16),
        heads=heads,
    )


def _trunk(x3d, p3d, w, norms, q_rank, kv_rank, late_weights):
    batch, seq, d = x3d.shape
    x = x3d.reshape(batch * seq, d)
    p = p3d.reshape(batch * seq, p3d.shape[-1])
    heads = w["heads"]

    xb, ssq = cast_sumsq(x)
    a, casted = ffn_up(xb, ssq, *w["ffn1_up"], weight_casts=() if late_weights else w["late_casts"])
    ffn1_wd, ffn2_wg, ffn2_wu, ffn2_wd = late_weights = late_weights or casted
    x, xb, ssq = matmul_residual(a, ffn1_wd, x, 0.5, 512, 512, "ffn_down", emit_norm_inputs=True)

    uv = matmul_normed(xb, ssq, w["w_uv"], F32, 1024, 512, "in_proj_uv")
    lat = matmul_normed(xb, ssq, w["w_lat"], F32, 512, w["w_lat"].shape[1], "in_proj_lat")
    gates = matmul_normed(xb, ssq, w["w_gates"], F32, 1024, 512, "in_proj_gates")

    gm = gmlp_gate(uv, norms["gm_v"], w["ws"], w["bs_b"])

    c1, c2 = _rope_tables(seq)
    q = q_proj(lat, norms["q"], w["w_uq_p"], c1, c2, q_rank, seq)
    k, v = kv_proj(lat, norms["kv"], w["w_k"], w["w_v"], c1, c2, q_rank, kv_rank, heads, seq)
    o = attention(q, k, v, batch, seq, heads)

    merged = gated_merge(gm, o, w["w_out_a"], w["w_out_b"], gates)
    x, xb, ssq = matmul_residual(merged, w["w_out"], x, 1.0, 1024, 512, "out_proj", emit_norm_inputs=True)

    a, _ = ffn_up(xb, ssq, ffn2_wg, ffn2_wu)
    x, xb, ssq = matmul_residual(a, ffn2_wd, x, 0.5, 512, 512, "ffn_down", emit_norm_inputs=True)

    t = ple_gate(xb, ssq, w["w_ple_gate"], p, w["w_ple_proj"])
    y = ple_add_final_norm(x, t, norms["ple_post"], norms["final"])
    return y.reshape(batch, seq, d), late_weights


def kernel(x_prompt, x_sample, p_prompt, p_sample, ffn1_norm, ffn1_wg, ffn1_wu, ffn1_wd, mix_norm, w_in, gm_v_norm, gm_ws, gm_bs, w_out_a, q_norm, w_uq, kv_norm, w_ukv, w_out_b, w_out, ffn2_norm, ffn2_wg, ffn2_wu, ffn2_wd, ple_gate_norm, w_ple_gate, w_ple_proj, ple_post_norm, final_norm):
    depth = ffn1_norm.shape[0]
    q_rank = q_norm.shape[1]
    kv_rank = kv_norm.shape[1]
    gm_width = gm_v_norm.shape[1]
    assert q_rank % kv_rank == 0 and (q_rank + kv_rank) % LANES == 0

    xs = [x_prompt, x_sample]
    for l in range(depth):
        w = _prepare_weights(ffn1_norm[l], ffn1_wg[l], ffn1_wu[l], ffn1_wd[l], mix_norm[l], w_in[l], gm_ws[l],
                             gm_bs[l], w_out_a[l], w_uq[l], w_ukv[l], w_out_b[l], w_out[l], ffn2_norm[l], ffn2_wg[l],
                             ffn2_wu[l], ffn2_wd[l], ple_gate_norm[l], w_ple_gate[l], w_ple_proj[l], q_rank, kv_rank,
                             gm_width)
        norms = dict(gm_v=gm_v_norm[l], q=q_norm[l], kv=kv_norm[l], ple_post=ple_post_norm[l], final=final_norm)
        assert depth == 1
        late_weights = None
        for g, p in enumerate([p_prompt, p_sample]):
            xs[g], late_weights = _trunk(xs[g], p[l], w, norms, q_rank, kv_rank, late_weights)
    return tuple(xs)
```

```python
import functools

import jax
import jax.numpy as jnp
import numpy as np
from jax import lax
from jax.experimental import pallas as pl
from jax.experimental.pallas import tpu as pltpu

F32 = jnp.float32
BF16 = jnp.bfloat16

RMS_EPS = 1e-6
ROPE_THETA = 10000.0
QK_NOPE_DIM = 128
QK_ROPE_DIM = 64
V_HEAD_DIM = 128
LANES = 128
HEAD_PAD = 2 * LANES
QK_LOG2_SCALE = float((QK_NOPE_DIM + QK_ROPE_DIM) ** -0.5 * np.log2(np.e))
MM_TILE = 1024
ATTN_UNROLL_CHUNKS = 8
VMEM_LIMIT = 56 * 1024 * 1024


def _cparams(*sem):
    return pltpu.CompilerParams(dimension_semantics=sem, vmem_limit_bytes=VMEM_LIMIT)


def _tile(n, want):
    if n <= want:
        return n
    t = want
    while n % t:
        t //= 2
    return t


def _sigmoid(x):
    return 1.0 / (1.0 + jnp.exp(-x))


def _gelu_tanh(x):
    c = np.float32(np.sqrt(2.0 / np.pi))
    return x * (0.5 * (1.0 + jnp.tanh(c * (x + 0.044715 * (x * x * x)))))


def _rms_scale(x):
    return lax.rsqrt(jnp.mean(x * x, axis=-1, keepdims=True) + RMS_EPS)


def _lane_partial_sumsq(x):
    x2 = x * x
    return sum(x2[:, c * LANES:(c + 1) * LANES] for c in range(x.shape[1] // LANES))


def _row_scale(ssq_ref, width):
    return lax.rsqrt(jnp.sum(ssq_ref[...], axis=-1, keepdims=True) / width + RMS_EPS)


def _cast_sumsq_kernel(x_ref, xb_ref, ssq_ref):
    x = x_ref[...]
    xb_ref[...] = x.astype(xb_ref.dtype)
    ssq_ref[...] = _lane_partial_sumsq(x)


def cast_sumsq(x):
    m, d = x.shape
    tm = _tile(m, 256)
    return pl.pallas_call(
        _cast_sumsq_kernel,
        grid=(m // tm,),
        in_specs=[pl.BlockSpec((tm, d), lambda i: (i, 0))],
        out_specs=[pl.BlockSpec((tm, d), lambda i: (i, 0)), pl.BlockSpec((tm, LANES), lambda i: (i, 0))],
        out_shape=[jax.ShapeDtypeStruct((m, d), BF16), jax.ShapeDtypeStruct((m, LANES), F32)],
        compiler_params=_cparams("parallel"),
        name="cast_sumsq",
    )(x)


def _mm_normed_kernel(xb_ref, ssq_ref, w_ref, o_ref, rs_ref):
    @pl.when(pl.program_id(1) == 0)
    def _():
        rs_ref[...] = _row_scale(ssq_ref, xb_ref.shape[1])

    acc = jnp.dot(xb_ref[...], w_ref[...], preferred_element_type=F32)
    o_ref[...] = (acc * rs_ref[...]).astype(o_ref.dtype)


def matmul_normed(xb, ssq, w, out_dtype, tm, tn, name):
    m, k = xb.shape
    n = w.shape[1]
    tm, tn = _tile(m, tm), _tile(n, tn)
    return pl.pallas_call(
        _mm_normed_kernel,
        grid=(m // tm, n // tn),
        in_specs=[
            pl.BlockSpec((tm, k), lambda i, j: (i, 0)),
            pl.BlockSpec((tm, LANES), lambda i, j: (i, 0)),
            pl.BlockSpec((k, tn), lambda i, j: (0, j)),
        ],
        out_specs=pl.BlockSpec((tm, tn), lambda i, j: (i, j)),
        out_shape=jax.ShapeDtypeStruct((m, n), out_dtype),
        scratch_shapes=[pltpu.VMEM((tm, 1), F32)],
        compiler_params=_cparams("parallel", "arbitrary"),
        name=name,
    )(xb, ssq, w)


def _mm_residual_kernel(a_ref, w_ref, x_ref, o_ref, *norm_refs, scale):
    acc = jnp.dot(a_ref[...], w_ref[...], preferred_element_type=F32)
    y = x_ref[...] + (acc if scale == 1.0 else scale * acc)
    o_ref[...] = y
    if norm_refs:
        yb_ref, ssq_ref = norm_refs
        yb_ref[...] = y.astype(yb_ref.dtype)

        @pl.when(pl.program_id(1) == 0)
        def _():
            ssq_ref[...] = jnp.zeros_like(ssq_ref)

        ssq_ref[...] += _lane_partial_sumsq(y)


def matmul_residual(a, w, x, scale, tm, tn, name, emit_norm_inputs=False):
    m, k = a.shape
    n = w.shape[1]
    tm, tn = _tile(m, tm), _tile(n, tn)
    tile = pl.BlockSpec((tm, tn), lambda i, j: (i, j))
    out_specs, out_shape = [tile], [jax.ShapeDtypeStruct((m, n), F32)]
    if emit_norm_inputs:
        out_specs += [tile, pl.BlockSpec((tm, LANES), lambda i, j: (i, 0))]
        out_shape += [jax.ShapeDtypeStruct((m, n), BF16), jax.ShapeDtypeStruct((m, LANES), F32)]
    out = pl.pallas_call(
        functools.partial(_mm_residual_kernel, scale=scale),
        grid=(m // tm, n // tn),
        in_specs=[pl.BlockSpec((tm, k), lambda i, j: (i, 0)), pl.BlockSpec((k, tn), lambda i, j: (0, j)), tile],
        out_specs=out_specs,
        out_shape=out_shape,
        compiler_params=_cparams("parallel", "arbitrary"),
        name=name,
    )(a, w, x)
    return out if emit_norm_inputs else out[0]


def _ffn_up_kernel(xb_ref, ssq_ref, wg_ref, wu_ref, *refs, row_splits, cast_has_gain):
    n_casts = len(cast_has_gain)
    n_cast_in = n_casts + sum(cast_has_gain)
    cast_in, o_ref, cast_out, rs_ref = refs[:n_cast_in], refs[n_cast_in], refs[n_cast_in + 1:-1], refs[-1]

    @pl.when(pl.program_id(1) == 0)
    def _():
        rs_ref[...] = _row_scale(ssq_ref, xb_ref.shape[1])

    rows = xb_ref.shape[0] // row_splits
    for r in range(row_splits):
        sl = slice(r * rows, (r + 1) * rows)
        xb = xb_ref[sl, :]
        rs = rs_ref[sl, :]
        g = jnp.dot(xb, wg_ref[...], preferred_element_type=F32) * rs
        u = jnp.dot(xb, wu_ref[...], preferred_element_type=F32) * rs
        o_ref[sl, :] = ((g * _sigmoid(g)) * u).astype(o_ref.dtype)

    pos = 0
    for c, has_gain in enumerate(cast_has_gain):
        w = cast_in[pos][...]
        if has_gain:
            w = cast_in[pos + 1][...] * w
        cast_out[c][...] = w.astype(cast_out[c].dtype)
        pos += 1 + has_gain


def ffn_up(xb, ssq, wg, wu, weight_casts=()):
    m, k = xb.shape
    n = wg.shape[1]
    tm, tn = _tile(m, 2048), _tile(n, 256)
    ni, nj = m // tm, n // tn
    in_specs = [
        pl.BlockSpec((tm, k), lambda i, j: (i, 0)),
        pl.BlockSpec((tm, LANES), lambda i, j: (i, 0)),
        pl.BlockSpec((k, tn), lambda i, j: (0, j)),
        pl.BlockSpec((k, tn), lambda i, j: (0, j)),
    ]
    out_specs = [pl.BlockSpec((tm, tn), lambda i, j: (i, j))]
    out_shape = [jax.ShapeDtypeStruct((m, n), BF16)]
    cast_args = []
    for w, gain in weight_casts:
        r, c = w.shape
        if r % ni == 0 and c % nj == 0:
            blk, idx, gidx = (r // ni, c // nj), (lambda i, j: (i, j)), (lambda i, j: (i, 0))
        else:
            blk, idx, gidx = (r // nj, c // ni), (lambda i, j: (j, i)), (lambda i, j: (j, 0))
        assert blk[0] * (r // blk[0]) == r and blk[1] * (c // blk[1]) == c and (r // blk[0]) * (c // blk[1]) == ni * nj
        in_specs.append(pl.BlockSpec(blk, idx))
        cast_args.append(w)
        if gain is not None:
            in_specs.append(pl.BlockSpec((blk[0], 1), gidx))
            cast_args.append(gain.reshape(r, 1))
        out_specs.append(pl.BlockSpec(blk, idx))
        out_shape.append(jax.ShapeDtypeStruct((r, c), BF16))
    out = pl.pallas_call(
        functools.partial(_ffn_up_kernel, row_splits=4 if tm % 1024 == 0 else 1,
                          cast_has_gain=tuple(g is not None for _, g in weight_casts)),
        grid=(ni, nj),
        in_specs=in_specs,
        out_specs=out_specs,
        out_shape=out_shape,
        scratch_shapes=[pltpu.VMEM((tm, 1), F32)],
        compiler_params=_cparams("parallel", "arbitrary"),
        name="ffn_up",
    )(xb, ssq, wg, wu, *cast_args)
    return out[0], out[1:]


def _gmlp_kernel(u_ref, v_ref, vg_ref, ws_ref, bs_ref, o_ref, vn_ref, *, chunk, groups, gdim):
    v = _gelu_tanh(v_ref[...])
    vn_ref[...] = ((v * _rms_scale(v)) * vg_ref[...]).astype(vn_ref.dtype)
    tm = u_ref.shape[0]
    for c in range(tm // chunk):
        rows = slice(c * chunk, (c + 1) * chunk)
        for g in range(groups):
            cols = slice(g * gdim, (g + 1) * gdim)
            mixed = jnp.dot(ws_ref[g], vn_ref[rows, cols], preferred_element_type=F32) + bs_ref[g]
            o_ref[rows, cols] = (_gelu_tanh(u_ref[rows, cols]) * mixed).astype(o_ref.dtype)


def gmlp_gate(uv, v_norm, ws, bs_b):
    m = uv.shape[0]
    groups, chunk, _ = ws.shape
    gdim = bs_b.shape[2]
    w = groups * gdim
    tm = _tile(m, 2 * chunk)
    return pl.pallas_call(
        functools.partial(_gmlp_kernel, chunk=chunk, groups=groups, gdim=gdim),
        grid=(m // tm,),
        in_specs=[
            pl.BlockSpec((tm, w), lambda i: (i, 0)),
            pl.BlockSpec((tm, w), lambda i: (i, 1)),
            pl.BlockSpec((1, w), lambda i: (0, 0)),
            pl.BlockSpec((groups, chunk, chunk), lambda i: (0, 0, 0)),
            pl.BlockSpec((groups, chunk, gdim), lambda i: (0, 0, 0)),
        ],
        out_specs=pl.BlockSpec((tm, w), lambda i: (i, 0)),
        out_shape=jax.ShapeDtypeStruct((m, w), BF16),
        scratch_shapes=[pltpu.VMEM((tm, w), BF16)],
        compiler_params=_cparams("parallel"),
        name="gmlp_gate",
    )(uv, uv, v_norm.reshape(1, w), ws, bs_b)


def _rope_block(blk, c1, c2):
    return blk * c1 + pltpu.roll(blk, LANES // 2, 1) * c2


def _q_proj_kernel(ql_ref, g_ref, w_ref, c1_ref, c2_ref, o_ref, hn_ref, *, heads_per_tile):
    @pl.when(pl.program_id(1) == 0)
    def _():
        x = ql_ref[...]
        hn_ref[...] = ((x * _rms_scale(x)) * g_ref[...]).astype(hn_ref.dtype)

    acc = jnp.dot(hn_ref[...], w_ref[...], preferred_element_type=F32)
    c1, c2 = c1_ref[...], c2_ref[...]
    for h in range(heads_per_tile):
        lo = h * HEAD_PAD
        o_ref[:, lo:lo + LANES] = (acc[:, lo:lo + LANES] * QK_LOG2_SCALE).astype(o_ref.dtype)
        roped = _rope_block(acc[:, lo + LANES:lo + HEAD_PAD], c1, c2)
        o_ref[:, lo + LANES:lo + HEAD_PAD] = (roped * QK_LOG2_SCALE).astype(o_ref.dtype)


def q_proj(lat, q_norm, w_uq_p, c1, c2, q_rank, seq):
    m = lat.shape[0]
    n = w_uq_p.shape[1]
    tm = _tile(seq, 1024)
    tn = n
    pos_blocks = seq // tm
    return pl.pallas_call(
        functools.partial(_q_proj_kernel, heads_per_tile=tn // HEAD_PAD),
        grid=(m // tm, n // tn),
        in_specs=[
            pl.BlockSpec((tm, q_rank), lambda i, j: (i, 0)),
            pl.BlockSpec((1, q_rank), lambda i, j: (0, 0)),
            pl.BlockSpec((q_rank, tn), lambda i, j: (0, j)),
            pl.BlockSpec((tm, LANES), lambda i, j: (i % pos_blocks, 0)),
            pl.BlockSpec((tm, LANES), lambda i, j: (i % pos_blocks, 0)),
        ],
        out_specs=pl.BlockSpec((tm, tn), lambda i, j: (i, j)),
        out_shape=jax.ShapeDtypeStruct((m, n), BF16),
        scratch_shapes=[pltpu.VMEM((tm, q_rank), BF16)],
        compiler_params=_cparams("parallel", "arbitrary"),
        name="q_proj",
    )(lat, q_norm.reshape(1, q_rank), w_uq_p, c1, c2)


def _kv_proj_kernel(kvl_ref, kr_ref, g_ref, wk_ref, wv_ref, c1_ref, c2_ref, k_ref, v_ref, *, heads):
    x = kvl_ref[...]
    hn = ((x * _rms_scale(x)) * g_ref[...]).astype(BF16)
    kn = jnp.dot(hn, wk_ref[...], preferred_element_type=F32)
    v_ref[...] = jnp.dot(hn, wv_ref[...], preferred_element_type=F32).astype(v_ref.dtype)
    kr = _rope_block(kr_ref[...], c1_ref[...], c2_ref[...]).astype(k_ref.dtype)
    for h in range(heads):
        lo = h * HEAD_PAD
        k_ref[:, lo:lo + LANES] = kn[:, h * QK_NOPE_DIM:(h + 1) * QK_NOPE_DIM].astype(k_ref.dtype)
        k_ref[:, lo + LANES:lo + HEAD_PAD] = kr


def kv_proj(lat, kv_norm, w_k, w_v, c1, c2, q_rank, kv_rank, heads, seq):
    m = lat.shape[0]
    tm = _tile(seq, 512)
    pos_blocks = seq // tm
    kv_blk = q_rank // kv_rank
    kr_blk = (q_rank + kv_rank) // LANES
    return pl.pallas_call(
        functools.partial(_kv_proj_kernel, heads=heads),
        grid=(m // tm,),
        in_specs=[
            pl.BlockSpec((tm, kv_rank), lambda i: (i, kv_blk)),
            pl.BlockSpec((tm, LANES), lambda i: (i, kr_blk)),
            pl.BlockSpec((1, kv_rank), lambda i: (0, 0)),
            pl.BlockSpec(w_k.shape, lambda i: (0, 0)),
            pl.BlockSpec(w_v.shape, lambda i: (0, 0)),
            pl.BlockSpec((tm, LANES), lambda i: (i % pos_blocks, 0)),
            pl.BlockSpec((tm, LANES), lambda i: (i % pos_blocks, 0)),
        ],
        out_specs=[
            pl.BlockSpec((tm, heads * HEAD_PAD), lambda i: (i, 0)),
            pl.BlockSpec((tm, heads * V_HEAD_DIM), lambda i: (i, 0)),
        ],
        out_shape=[
            jax.ShapeDtypeStruct((m, heads * HEAD_PAD), BF16),
            jax.ShapeDtypeStruct((m, heads * V_HEAD_DIM), BF16),
        ],
        compiler_params=_cparams("parallel"),
        name="kv_proj",
    )(lat, lat, kv_norm.reshape(1, kv_rank), w_k, w_v, c1, c2)


def _attn_kernel(q_ref, k_ref, v_ref, o_ref, *, tq, tk):
    seq = k_ref.shape[0]
    nq, nk = seq // tq, seq // tk
    kv_unroll = min(nk, ATTN_UNROLL_CHUNKS)
    q_unroll = max(1, min(nq, ATTN_UNROLL_CHUNKS // kv_unroll))

    def q_body(qi, _):
        q0 = pl.multiple_of(qi * tq, tq)
        q = q_ref[pl.ds(q0, tq), :]

        def kv_body(ki, carry):
            m, l, acc = carry
            k0 = pl.multiple_of(ki * tk, tk)
            k = k_ref[pl.ds(k0, tk), :]
            s = lax.dot_general(q, k, (((1,), (1,)), ((), ())), preferred_element_type=F32)
            m_new = jnp.maximum(m, jnp.max(s, axis=-1, keepdims=True))
            p = jnp.exp2(s - m_new)
            alpha = jnp.exp2(m - m_new)
            l = alpha * l + jnp.sum(p, axis=-1, keepdims=True)
            pv = jnp.dot(p.astype(BF16), v_ref[pl.ds(k0, tk), :], preferred_element_type=F32)
            return m_new, l, alpha * acc + pv

        init = (jnp.full((tq, 1), -jnp.inf, F32), jnp.zeros((tq, 1), F32), jnp.zeros((tq, V_HEAD_DIM), F32))
        _, l, acc = lax.fori_loop(0, nk, kv_body, init, unroll=kv_unroll)
        o_ref[pl.ds(q0, tq), :] = (acc / l).astype(o_ref.dtype)
        return 0

    lax.fori_loop(0, nq, q_body, 0, unroll=q_unroll)


def attention(q, k, v, batch, seq, heads):
    m = q.shape[0]
    tq, tk = _tile(seq, 1024), _tile(seq, 1024)
    return pl.pallas_call(
        functools.partial(_attn_kernel, tq=tq, tk=tk),
        grid=(batch, heads),
        in_specs=[
            pl.BlockSpec((seq, HEAD_PAD), lambda b, h: (b, h)),
            pl.BlockSpec((seq, HEAD_PAD), lambda b, h: (b, h)),
            pl.BlockSpec((seq, V_HEAD_DIM), lambda b, h: (b, h)),
        ],
        out_specs=pl.BlockSpec((seq, V_HEAD_DIM), lambda b, h: (b, h)),
        out_shape=jax.ShapeDtypeStruct((m, heads * V_HEAD_DIM), BF16),
        compiler_params=_cparams("parallel", "parallel"),
        name="attention",
    )(q, k, v)


def _merge_kernel(a_ref, b_ref, wa_ref, wb_ref, ga_ref, gb_ref, o_ref):
    ya = jnp.dot(a_ref[...], wa_ref[...], preferred_element_type=F32)
    yb = jnp.dot(b_ref[...], wb_ref[...], preferred_element_type=F32)
    o_ref[...] = (_sigmoid(ga_ref[...]) * ya + _sigmoid(gb_ref[...]) * yb).astype(o_ref.dtype)


def gated_merge(a, b, wa, wb, gates):
    m, ka = a.shape
    kb = b.shape[1]
    n = wa.shape[1]
    tm, tn = _tile(m, MM_TILE), _tile(n, MM_TILE // 2)
    nj = n // tn
    return pl.pallas_call(
        _merge_kernel,
        grid=(m // tm, nj),
        in_specs=[
            pl.BlockSpec((tm, ka), lambda i, j: (i, 0)),
            pl.BlockSpec((tm, kb), lambda i, j: (i, 0)),
            pl.BlockSpec((ka, tn), lambda i, j: (0, j)),
            pl.BlockSpec((kb, tn), lambda i, j: (0, j)),
            pl.BlockSpec((tm, tn), lambda i, j: (i, j)),
            pl.BlockSpec((tm, tn), lambda i, j: (i, j + nj)),
        ],
        out_specs=pl.BlockSpec((tm, tn), lambda i, j: (i, j)),
        out_shape=jax.ShapeDtypeStruct((m, n), BF16),
        compiler_params=_cparams("parallel", "parallel"),
        name="gated_merge",
    )(a, b, wa, wb, gates, gates)


def _ple_gate_kernel(xb_ref, ssq_ref, wg_ref, p_ref, wp_ref, o_ref, rs_ref):
    @pl.when(pl.program_id(1) == 0)
    def _():
        rs_ref[...] = _row_scale(ssq_ref, xb_ref.shape[1])

    g = jnp.dot(xb_ref[...], wg_ref[...], preferred_element_type=F32) * rs_ref[...]
    pp = jnp.dot(p_ref[...].astype(BF16), wp_ref[...], preferred_element_type=F32)
    o_ref[...] = _sigmoid(g) * pp


def ple_gate(xb, ssq, wg, p, wp):
    m, k = xb.shape
    kp = p.shape[1]
    n = wg.shape[1]
    tm, tn = _tile(m, MM_TILE), _tile(n, MM_TILE)
    return pl.pallas_call(
        _ple_gate_kernel,
        grid=(m // tm, n // tn),
        in_specs=[
            pl.BlockSpec((tm, k), lambda i, j: (i, 0)),
            pl.BlockSpec((tm, LANES), lambda i, j: (i, 0)),
            pl.BlockSpec((k, tn), lambda i, j: (0, j)),
            pl.BlockSpec((tm, kp), lambda i, j: (i, 0)),
            pl.BlockSpec((kp, tn), lambda i, j: (0, j)),
        ],
        out_specs=pl.BlockSpec((tm, tn), lambda i, j: (i, j)),
        out_shape=jax.ShapeDtypeStruct((m, n), F32),
        scratch_shapes=[pltpu.VMEM((tm, 1), F32)],
        compiler_params=_cparams("parallel", "arbitrary"),
        name="ple_gate",
    )(xb, ssq, wg, p, wp)


def _final_kernel(x_ref, t_ref, gp_ref, gf_ref, o_ref):
    t = t_ref[...]
    x = x_ref[...] + (t * _rms_scale(t)) * gp_ref[...]
    o_ref[...] = (x * _rms_scale(x)) * gf_ref[...]


def ple_add_final_norm(x, t, post_g, final_g):
    m, d = x.shape
    tm = _tile(m, 256)
    row = pl.BlockSpec((tm, d), lambda i: (i, 0))
    vec = pl.BlockSpec((1, d), lambda i: (0, 0))
    return pl.pallas_call(
        _final_kernel,
        grid=(m // tm,),
        in_specs=[row, row, vec, vec],
        out_specs=row,
        out_shape=jax.ShapeDtypeStruct((m, d), F32),
        compiler_params=_cparams("parallel"),
        name="ple_add_final_norm",
    )(x, t, post_g.reshape(1, d), final_g.reshape(1, d))


def _rope_tables(seq):
    inv = 1.0 / (ROPE_THETA ** (jnp.arange(0, QK_ROPE_DIM, 2, dtype=F32) / QK_ROPE_DIM))
    ang = jnp.arange(seq, dtype=F32)[:, None] * inv[None, :]
    cos, sin = jnp.cos(ang), jnp.sin(ang)
    zero = jnp.zeros((seq, LANES - QK_ROPE_DIM), F32)
    return jnp.concatenate([cos, cos, zero], axis=1), jnp.concatenate([-sin, sin, zero], axis=1)


def _rotate_half_cols(w):
    half = QK_ROPE_DIM // 2
    return jnp.concatenate([w[..., half:], w[..., :half]], axis=-1)


def _prepare_weights(ffn1_norm, ffn1_wg, ffn1_wu, ffn1_wd, mix_norm, w_in, gm_ws, gm_bs, w_out_a, w_uq, w_ukv,
                     w_out_b, w_out, ffn2_norm, ffn2_wg, ffn2_wu, ffn2_wd, ple_gate_norm, w_ple_gate, w_ple_proj,
                     q_rank, kv_rank, gm_width):
    def up(w, gain):
        return (gain[:, None] * w).astype(BF16)

    def in_cols(lo, hi):
        return up(w_in[:, lo:hi], mix_norm)

    o_q = 2 * gm_width
    o_kv = o_q + q_rank
    o_kr = o_kv + kv_rank
    o_gate = o_kr + QK_ROPE_DIM
    w_kr = in_cols(o_kr, o_gate)
    w_lat = jnp.concatenate([in_cols(o_q, o_kr), w_kr, _rotate_half_cols(w_kr)], axis=1)

    heads = w_uq.shape[1] // (QK_NOPE_DIM + QK_ROPE_DIM)
    uq = w_uq.reshape(q_rank, heads, QK_NOPE_DIM + QK_ROPE_DIM)
    uq_rope = uq[:, :, QK_NOPE_DIM:]
    w_uq_p = jnp.concatenate([uq[:, :, :QK_NOPE_DIM], uq_rope, _rotate_half_cols(uq_rope)], axis=2)
    ukv = w_ukv.reshape(kv_rank, heads, QK_NOPE_DIM + V_HEAD_DIM)

    gdim = gm_width // gm_ws.shape[0]
    return dict(
        ffn1_up=(up(ffn1_wg, ffn1_norm), up(ffn1_wu, ffn1_norm)),
        late_casts=((ffn1_wd, None), (ffn2_wg, ffn2_norm), (ffn2_wu, ffn2_norm), (ffn2_wd, None)),
        w_uv=in_cols(0, o_q),
        w_lat=w_lat,
        w_gates=in_cols(o_gate, w_in.shape[1]),
        ws=gm_ws.astype(BF16),
        bs_b=jnp.broadcast_to(gm_bs[:, :, None], gm_bs.shape + (gdim,)).astype(F32),
        w_out_a=w_out_a.astype(BF16),
        w_uq_p=w_uq_p.reshape(q_rank, heads * HEAD_PAD).astype(BF16),
        w_k=ukv[:, :, :QK_NOPE_DIM].reshape(kv_rank, heads * QK_NOPE_DIM).astype(BF16),
        w_v=ukv[:, :, QK_NOPE_DIM:].reshape(kv_rank, heads * V_HEAD_DIM).astype(BF16),
        w_out_b=w_out_b.astype(BF16),
        w_out=w_out.astype(BF16),
        w_ple_gate=up(w_ple_gate, ple_gate_norm),
        w_ple_proj=w_ple_proj.astype(BF16),
        heads=heads,
    )


def _trunk(x3d, p3d, w, norms, q_rank, kv_rank, late_weights):
    batch, seq, d = x3d.shape
    x = x3d.reshape(batch * seq, d)
    p = p3d.reshape(batch * seq, p3d.shape[-1])
    heads = w["heads"]

    xb, ssq = cast_sumsq(x)
    a, casted = ffn_up(xb, ssq, *w["ffn1_up"], weight_casts=() if late_weights else w["late_casts"])
    ffn1_wd, ffn2_wg, ffn2_wu, ffn2_wd = late_weights = late_weights or casted
    x, xb, ssq = matmul_residual(a, ffn1_wd, x, 0.5, 512, 512, "ffn_down", emit_norm_inputs=True)

    uv = matmul_normed(xb, ssq, w["w_uv"], F32, MM_TILE, MM_TILE, "in_proj_uv")
    lat = matmul_normed(xb, ssq, w["w_lat"], F32, 512, w["w_lat"].shape[1], "in_proj_lat")
    gates = matmul_normed(xb, ssq, w["w_gates"], F32, MM_TILE, MM_TILE, "in_proj_gates")

    gm = gmlp_gate(uv, norms["gm_v"], w["ws"], w["bs_b"])

    c1, c2 = _rope_tables(seq)
    q = q_proj(lat, norms["q"], w["w_uq_p"], c1, c2, q_rank, seq)
    k, v = kv_proj(lat, norms["kv"], w["w_k"], w["w_v"], c1, c2, q_rank, kv_rank, heads, seq)
    o = attention(q, k, v, batch, seq, heads)

    merged = gated_merge(gm, o, w["w_out_a"], w["w_out_b"], gates)
    x, xb, ssq = matmul_residual(merged, w["w_out"], x, 1.0, MM_TILE, MM_TILE // 2, "out_proj", emit_norm_inputs=True)

    a, _ = ffn_up(xb, ssq, ffn2_wg, ffn2_wu)
    x, xb, ssq = matmul_residual(a, ffn2_wd, x, 0.5, 512, 512, "ffn_down", emit_norm_inputs=True)

    t = ple_gate(xb, ssq, w["w_ple_gate"], p, w["w_ple_proj"])
    y = ple_add_final_norm(x, t, norms["ple_post"], norms["final"])
    return y.reshape(batch, seq, d), late_weights


def kernel(x_prompt, x_sample, p_prompt, p_sample, ffn1_norm, ffn1_wg, ffn1_wu, ffn1_wd, mix_norm, w_in, gm_v_norm, gm_ws, gm_bs, w_out_a, q_norm, w_uq, kv_norm, w_ukv, w_out_b, w_out, ffn2_norm, ffn2_wg, ffn2_wu, ffn2_wd, ple_gate_norm, w_ple_gate, w_ple_proj, ple_post_norm, final_norm):
    depth = ffn1_norm.shape[0]
    q_rank = q_norm.shape[1]
    kv_rank = kv_norm.shape[1]
    gm_width = gm_v_norm.shape[1]
    assert q_rank % kv_rank == 0 and (q_rank + kv_rank) % LANES == 0

    xs = [x_prompt, x_sample]
    for l in range(depth):
        w = _prepare_weights(ffn1_norm[l], ffn1_wg[l], ffn1_wu[l], ffn1_wd[l], mix_norm[l], w_in[l], gm_ws[l],
                             gm_bs[l], w_out_a[l], w_uq[l], w_ukv[l], w_out_b[l], w_out[l], ffn2_norm[l], ffn2_wg[l],
                             ffn2_wu[l], ffn2_wd[l], ple_gate_norm[l], w_ple_gate[l], w_ple_proj[l], q_rank, kv_rank,
                             gm_width)
        norms = dict(gm_v=gm_v_norm[l], q=q_norm[l], kv=kv_norm[l], ple_post=ple_post_norm[l], final=final_norm)
        assert depth == 1
        late_weights = None
        for g, p in enumerate([p_prompt, p_sample]):
            xs[g], late_weights = _trunk(xs[g], p[l], w, norms, q_rank, kv_rank, late_weights)
    return tuple(xs)
```

```python
import functools

import jax
import jax.numpy as jnp
import numpy as np
from jax import lax
from jax.experimental import pallas as pl
from jax.experimental.pallas import tpu as pltpu

F32 = jnp.float32
BF16 = jnp.bfloat16

RMS_EPS = 1e-6
ROPE_THETA = 10000.0
QK_NOPE_DIM = 128
QK_ROPE_DIM = 64
V_HEAD_DIM = 128
LANES = 128
HEAD_PAD = 2 * LANES
QK_LOG2_SCALE = float((QK_NOPE_DIM + QK_ROPE_DIM) ** -0.5 * np.log2(np.e))
MM_TILE = 1024
ATTN_UNROLL_CHUNKS = 8
VMEM_LIMIT = 56 * 1024 * 1024


def _cparams(*sem):
    return pltpu.CompilerParams(dimension_semantics=sem, vmem_limit_bytes=VMEM_LIMIT)


def _tile(n, want):
    if n <= want:
        return n
    t = want
    while n % t:
        t //= 2
    return t


def _sigmoid(x):
    return 1.0 / (1.0 + jnp.exp(-x))


def _gelu_tanh(x):
    c = np.float32(np.sqrt(2.0 / np.pi))
    return x * (0.5 * (1.0 + jnp.tanh(c * (x + 0.044715 * (x * x * x)))))


def _rms_scale(x):
    return lax.rsqrt(jnp.mean(x * x, axis=-1, keepdims=True) + RMS_EPS)


def _lane_partial_sumsq(x):
    x2 = x * x
    return sum(x2[:, c * LANES:(c + 1) * LANES] for c in range(x.shape[1] // LANES))


def _row_scale(ssq_ref, width):
    return lax.rsqrt(jnp.sum(ssq_ref[...], axis=-1, keepdims=True) / width + RMS_EPS)


def _cast_sumsq_kernel(x_ref, xb_ref, ssq_ref):
    x = x_ref[...]
    xb_ref[...] = x.astype(xb_ref.dtype)
    ssq_ref[...] = _lane_partial_sumsq(x)


def cast_sumsq(x):
    m, d = x.shape
    tm = _tile(m, 256)
    return pl.pallas_call(
        _cast_sumsq_kernel,
        grid=(m // tm,),
        in_specs=[pl.BlockSpec((tm, d), lambda i: (i, 0))],
        out_specs=[pl.BlockSpec((tm, d), lambda i: (i, 0)), pl.BlockSpec((tm, LANES), lambda i: (i, 0))],
        out_shape=[jax.ShapeDtypeStruct((m, d), BF16), jax.ShapeDtypeStruct((m, LANES), F32)],
        compiler_params=_cparams("parallel"),
        name="cast_sumsq",
    )(x)


def _mm_normed_kernel(xb_ref, ssq_ref, w_ref, o_ref, rs_ref):
    @pl.when(pl.program_id(1) == 0)
    def _():
        rs_ref[...] = _row_scale(ssq_ref, xb_ref.shape[1])

    acc = lax.dot_general(xb_ref[...], w_ref[...], (((1,), (1,)), ((), ())), preferred_element_type=F32)
    o_ref[...] = (acc * rs_ref[...]).astype(o_ref.dtype)


def matmul_normed(xb, ssq, w_t, out_dtype, tm, tn, name):
    m, k = xb.shape
    n = w_t.shape[0]
    tm, tn = _tile(m, tm), _tile(n, tn)
    return pl.pallas_call(
        _mm_normed_kernel,
        grid=(m // tm, n // tn),
        in_specs=[
            pl.BlockSpec((tm, k), lambda i, j: (i, 0)),
            pl.BlockSpec((tm, LANES), lambda i, j: (i, 0)),
            pl.BlockSpec((tn, k), lambda i, j: (j, 0)),
        ],
        out_specs=pl.BlockSpec((tm, tn), lambda i, j: (i, j)),
        out_shape=jax.ShapeDtypeStruct((m, n), out_dtype),
        scratch_shapes=[pltpu.VMEM((tm, 1), F32)],
        compiler_params=_cparams("parallel", "arbitrary"),
        name=name,
    )(xb, ssq, w_t)


def _mm_residual_kernel(a_ref, w_ref, x_ref, o_ref, *norm_refs, scale):
    acc = jnp.dot(a_ref[...], w_ref[...], preferred_element_type=F32)
    y = x_ref[...] + (acc if scale == 1.0 else scale * acc)
    o_ref[...] = y
    if norm_refs:
        yb_ref, ssq_ref = norm_refs
        yb_ref[...] = y.astype(yb_ref.dtype)

        @pl.when(pl.program_id(1) == 0)
        def _():
            ssq_ref[...] = jnp.zeros_like(ssq_ref)

        ssq_ref[...] += _lane_partial_sumsq(y)


def matmul_residual(a, w, x, scale, tm, tn, name, emit_norm_inputs=False):
    m, k = a.shape
    n = w.shape[1]
    tm, tn = _tile(m, tm), _tile(n, tn)
    tile = pl.BlockSpec((tm, tn), lambda i, j: (i, j))
    out_specs, out_shape = [tile], [jax.ShapeDtypeStruct((m, n), F32)]
    if emit_norm_inputs:
        out_specs += [tile, pl.BlockSpec((tm, LANES), lambda i, j: (i, 0))]
        out_shape += [jax.ShapeDtypeStruct((m, n), BF16), jax.ShapeDtypeStruct((m, LANES), F32)]
    out = pl.pallas_call(
        functools.partial(_mm_residual_kernel, scale=scale),
        grid=(m // tm, n // tn),
        in_specs=[pl.BlockSpec((tm, k), lambda i, j: (i, 0)), pl.BlockSpec((k, tn), lambda i, j: (0, j)), tile],
        out_specs=out_specs,
        out_shape=out_shape,
        compiler_params=_cparams("parallel", "arbitrary"),
        name=name,
    )(a, w, x)
    return out if emit_norm_inputs else out[0]


def _ffn_up_kernel(xb_ref, ssq_ref, wg_ref, wu_ref, *refs, row_splits, cast_has_gain):
    n_casts = len(cast_has_gain)
    n_cast_in = n_casts + sum(cast_has_gain)
    cast_in, o_ref, cast_out, rs_ref = refs[:n_cast_in], refs[n_cast_in], refs[n_cast_in + 1:-1], refs[-1]

    @pl.when(pl.program_id(1) == 0)
    def _():
        rs_ref[...] = _row_scale(ssq_ref, xb_ref.shape[1])

    rows = xb_ref.shape[0] // row_splits
    for r in range(row_splits):
        sl = slice(r * rows, (r + 1) * rows)
        xb = xb_ref[sl, :]
        rs = rs_ref[sl, :]
        g = jnp.dot(xb, wg_ref[...], preferred_element_type=F32) * rs
        u = jnp.dot(xb, wu_ref[...], preferred_element_type=F32) * rs
        o_ref[sl, :] = ((g * _sigmoid(g)) * u).astype(o_ref.dtype)

    pos = 0
    for c, has_gain in enumerate(cast_has_gain):
        w = cast_in[pos][...]
        if has_gain:
            w = cast_in[pos + 1][...] * w
        cast_out[c][...] = w.astype(cast_out[c].dtype)
        pos += 1 + has_gain


def ffn_up(xb, ssq, wg, wu, weight_casts=()):
    m, k = xb.shape
    n = wg.shape[1]
    tm, tn = _tile(m, 2048), _tile(n, 256)
    ni, nj = m // tm, n // tn
    in_specs = [
        pl.BlockSpec((tm, k), lambda i, j: (i, 0)),
        pl.BlockSpec((tm, LANES), lambda i, j: (i, 0)),
        pl.BlockSpec((k, tn), lambda i, j: (0, j)),
        pl.BlockSpec((k, tn), lambda i, j: (0, j)),
    ]
    out_specs = [pl.BlockSpec((tm, tn), lambda i, j: (i, j))]
    out_shape = [jax.ShapeDtypeStruct((m, n), BF16)]
    cast_args = []
    for w, gain in weight_casts:
        r, c = w.shape
        if r % ni == 0 and c % nj == 0:
            blk, idx, gidx = (r // ni, c // nj), (lambda i, j: (i, j)), (lambda i, j: (i, 0))
        else:
            blk, idx, gidx = (r // nj, c // ni), (lambda i, j: (j, i)), (lambda i, j: (j, 0))
        assert blk[0] * (r // blk[0]) == r and blk[1] * (c // blk[1]) == c and (r // blk[0]) * (c // blk[1]) == ni * nj
        in_specs.append(pl.BlockSpec(blk, idx))
        cast_args.append(w)
        if gain is not None:
            in_specs.append(pl.BlockSpec((blk[0], 1), gidx))
            cast_args.append(gain.reshape(r, 1))
        out_specs.append(pl.BlockSpec(blk, idx))
        out_shape.append(jax.ShapeDtypeStruct((r, c), BF16))
    out = pl.pallas_call(
        functools.partial(_ffn_up_kernel, row_splits=4 if tm % 1024 == 0 else 1,
                          cast_has_gain=tuple(g is not None for _, g in weight_casts)),
        grid=(ni, nj),
        in_specs=in_specs,
        out_specs=out_specs,
        out_shape=out_shape,
        scratch_shapes=[pltpu.VMEM((tm, 1), F32)],
        compiler_params=_cparams("parallel", "arbitrary"),
        name="ffn_up",
    )(xb, ssq, wg, wu, *cast_args)
    return out[0], out[1:]


def _gmlp_kernel(u_ref, v_ref, vg_ref, ws_ref, bs_ref, o_ref, vn_ref, *, chunk, groups, gdim):
    v = _gelu_tanh(v_ref[...])
    vn_ref[...] = ((v * _rms_scale(v)) * vg_ref[...]).astype(vn_ref.dtype)
    tm = u_ref.shape[0]
    for c in range(tm // chunk):
        rows = slice(c * chunk, (c + 1) * chunk)
        for g in range(groups):
            cols = slice(g * gdim, (g + 1) * gdim)
            mixed = jnp.dot(ws_ref[g], vn_ref[rows, cols], preferred_element_type=F32) + bs_ref[g]
            o_ref[rows, cols] = (_gelu_tanh(u_ref[rows, cols]) * mixed).astype(o_ref.dtype)


def gmlp_gate(uv, v_norm, ws, bs_b):
    m = uv.shape[0]
    groups, chunk, _ = ws.shape
    gdim = bs_b.shape[2]
    w = groups * gdim
    tm = _tile(m, 2 * chunk)
    return pl.pallas_call(
        functools.partial(_gmlp_kernel, chunk=chunk, groups=groups, gdim=gdim),
        grid=(m // tm,),
        in_specs=[
            pl.BlockSpec((tm, w), lambda i: (i, 0)),
            pl.BlockSpec((tm, w), lambda i: (i, 1)),
            pl.BlockSpec((1, w), lambda i: (0, 0)),
            pl.BlockSpec((groups, chunk, chunk), lambda i: (0, 0, 0)),
            pl.BlockSpec((groups, chunk, gdim), lambda i: (0, 0, 0)),
        ],
        out_specs=pl.BlockSpec((tm, w), lambda i: (i, 0)),
        out_shape=jax.ShapeDtypeStruct((m, w), BF16),
        scratch_shapes=[pltpu.VMEM((tm, w), BF16)],
        compiler_params=_cparams("parallel"),
        name="gmlp_gate",
    )(uv, uv, v_norm.reshape(1, w), ws, bs_b)


def _rope_block(blk, c1, c2):
    return blk * c1 + pltpu.roll(blk, LANES // 2, 1) * c2


def _q_proj_kernel(ql_ref, g_ref, w_ref, c1_ref, c2_ref, o_ref, hn_ref, *, heads_per_tile):
    @pl.when(pl.program_id(1) == 0)
    def _():
        x = ql_ref[...]
        hn_ref[...] = ((x * _rms_scale(x)) * g_ref[...]).astype(hn_ref.dtype)

    acc = jnp.dot(hn_ref[...], w_ref[...], preferred_element_type=F32)
    c1, c2 = c1_ref[...], c2_ref[...]
    for h in range(heads_per_tile):
        lo = h * HEAD_PAD
        o_ref[:, lo:lo + LANES] = (acc[:, lo:lo + LANES] * QK_LOG2_SCALE).astype(o_ref.dtype)
        roped = _rope_block(acc[:, lo + LANES:lo + HEAD_PAD], c1, c2)
        o_ref[:, lo + LANES:lo + HEAD_PAD] = (roped * QK_LOG2_SCALE).astype(o_ref.dtype)


def q_proj(lat, q_norm, w_uq_p, c1, c2, q_rank, seq):
    m = lat.shape[0]
    n = w_uq_p.shape[1]
    tm = _tile(seq, 1024)
    tn = n
    pos_blocks = seq // tm
    return pl.pallas_call(
        functools.partial(_q_proj_kernel, heads_per_tile=tn // HEAD_PAD),
        grid=(m // tm, n // tn),
        in_specs=[
            pl.BlockSpec((tm, q_rank), lambda i, j: (i, 0)),
            pl.BlockSpec((1, q_rank), lambda i, j: (0, 0)),
            pl.BlockSpec((q_rank, tn), lambda i, j: (0, j)),
            pl.BlockSpec((tm, LANES), lambda i, j: (i % pos_blocks, 0)),
            pl.BlockSpec((tm, LANES), lambda i, j: (i % pos_blocks, 0)),
        ],
        out_specs=pl.BlockSpec((tm, tn), lambda i, j: (i, j)),
        out_shape=jax.ShapeDtypeStruct((m, n), BF16),
        scratch_shapes=[pltpu.VMEM((tm, q_rank), BF16)],
        compiler_params=_cparams("parallel", "arbitrary"),
        name="q_proj",
    )(lat, q_norm.reshape(1, q_rank), w_uq_p, c1, c2)


def _kv_proj_kernel(kvl_ref, kr_ref, g_ref, wk_ref, wv_ref, c1_ref, c2_ref, k_ref, v_ref, *, heads):
    x = kvl_ref[...]
    hn = ((x * _rms_scale(x)) * g_ref[...]).astype(BF16)
    kn = jnp.dot(hn, wk_ref[...], preferred_element_type=F32)
    v_ref[...] = jnp.dot(hn, wv_ref[...], preferred_element_type=F32).astype(v_ref.dtype)
    kr = _rope_block(kr_ref[...], c1_ref[...], c2_ref[...]).astype(k_ref.dtype)
    for h in range(heads):
        lo = h * HEAD_PAD
        k_ref[:, lo:lo + LANES] = kn[:, h * QK_NOPE_DIM:(h + 1) * QK_NOPE_DIM].astype(k_ref.dtype)
        k_ref[:, lo + LANES:lo + HEAD_PAD] = kr


def kv_proj(lat, kv_norm, w_k, w_v, c1, c2, q_rank, kv_rank, heads, seq):
    m = lat.shape[0]
    tm = _tile(seq, 512)
    pos_blocks = seq // tm
    kv_blk = q_rank // kv_rank
    kr_blk = (q_rank + kv_rank) // LANES
    return pl.pallas_call(
        functools.partial(_kv_proj_kernel, heads=heads),
        grid=(m // tm,),
        in_specs=[
            pl.BlockSpec((tm, kv_rank), lambda i: (i, kv_blk)),
            pl.BlockSpec((tm, LANES), lambda i: (i, kr_blk)),
            pl.BlockSpec((1, kv_rank), lambda i: (0, 0)),
            pl.BlockSpec(w_k.shape, lambda i: (0, 0)),
            pl.BlockSpec(w_v.shape, lambda i: (0, 0)),
            pl.BlockSpec((tm, LANES), lambda i: (i % pos_blocks, 0)),
            pl.BlockSpec((tm, LANES), lambda i: (i % pos_blocks, 0)),
        ],
        out_specs=[
            pl.BlockSpec((tm, heads * HEAD_PAD), lambda i: (i, 0)),
            pl.BlockSpec((tm, heads * V_HEAD_DIM), lambda i: (i, 0)),
        ],
        out_shape=[
            jax.ShapeDtypeStruct((m, heads * HEAD_PAD), BF16),
            jax.ShapeDtypeStruct((m, heads * V_HEAD_DIM), BF16),
        ],
        compiler_params=_cparams("parallel"),
        name="kv_proj",
    )(lat, lat, kv_norm.reshape(1, kv_rank), w_k, w_v, c1, c2)


def _attn_kernel(q_ref, k_ref, v_ref, o_ref, *, tq, tk):
    seq = k_ref.shape[0]
    nq, nk = seq // tq, seq // tk
    kv_unroll = min(nk, ATTN_UNROLL_CHUNKS)
    q_unroll = max(1, min(nq, ATTN_UNROLL_CHUNKS // kv_unroll))

    def q_body(qi, _):
        q0 = pl.multiple_of(qi * tq, tq)
        q = q_ref[pl.ds(q0, tq), :]

        def kv_body(ki, carry):
            m, l, acc = carry
            k0 = pl.multiple_of(ki * tk, tk)
            k = k_ref[pl.ds(k0, tk), :]
            s = lax.dot_general(q, k, (((1,), (1,)), ((), ())), preferred_element_type=F32)
            m_new = jnp.maximum(m, jnp.max(s, axis=-1, keepdims=True))
            p = jnp.exp2(s - m_new)
            alpha = jnp.exp2(m - m_new)
            l = alpha * l + jnp.sum(p, axis=-1, keepdims=True)
            pv = jnp.dot(p.astype(BF16), v_ref[pl.ds(k0, tk), :], preferred_element_type=F32)
            return m_new, l, alpha * acc + pv

        init = (jnp.full((tq, 1), -jnp.inf, F32), jnp.zeros((tq, 1), F32), jnp.zeros((tq, V_HEAD_DIM), F32))
        _, l, acc = lax.fori_loop(0, nk, kv_body, init, unroll=kv_unroll)
        o_ref[pl.ds(q0, tq), :] = (acc / l).astype(o_ref.dtype)
        return 0

    lax.fori_loop(0, nq, q_body, 0, unroll=q_unroll)


def attention(q, k, v, batch, seq, heads):
    m = q.shape[0]
    tq, tk = _tile(seq, 1024), _tile(seq, 1024)
    return pl.pallas_call(
        functools.partial(_attn_kernel, tq=tq, tk=tk),
        grid=(batch, heads),
        in_specs=[
            pl.BlockSpec((seq, HEAD_PAD), lambda b, h: (b, h)),
            pl.BlockSpec((seq, HEAD_PAD), lambda b, h: (b, h)),
            pl.BlockSpec((seq, V_HEAD_DIM), lambda b, h: (b, h)),
        ],
        out_specs=pl.BlockSpec((seq, V_HEAD_DIM), lambda b, h: (b, h)),
        out_shape=jax.ShapeDtypeStruct((m, heads * V_HEAD_DIM), BF16),
        compiler_params=_cparams("parallel", "parallel"),
        name="attention",
    )(q, k, v)


def _merge_kernel(a_ref, b_ref, wa_ref, wb_ref, ga_ref, gb_ref, o_ref):
    ya = jnp.dot(a_ref[...], wa_ref[...], preferred_element_type=F32)
    yb = jnp.dot(b_ref[...], wb_ref[...], preferred_element_type=F32)
    o_ref[...] = (_sigmoid(ga_ref[...]) * ya + _sigmoid(gb_ref[...]) * yb).astype(o_ref.dtype)


def gated_merge(a, b, wa, wb, gates):
    m, ka = a.shape
    kb = b.shape[1]
    n = wa.shape[1]
    tm, tn = _tile(m, MM_TILE), _tile(n, MM_TILE // 2)
    nj = n // tn
    return pl.pallas_call(
        _merge_kernel,
        grid=(m // tm, nj),
        in_specs=[
            pl.BlockSpec((tm, ka), lambda i, j: (i, 0)),
            pl.BlockSpec((tm, kb), lambda i, j: (i, 0)),
            pl.BlockSpec((ka, tn), lambda i, j: (0, j)),
            pl.BlockSpec((kb, tn), lambda i, j: (0, j)),
            pl.BlockSpec((tm, tn), lambda i, j: (i, j)),
            pl.BlockSpec((tm, tn), lambda i, j: (i, j + nj)),
        ],
        out_specs=pl.BlockSpec((tm, tn), lambda i, j: (i, j)),
        out_shape=jax.ShapeDtypeStruct((m, n), BF16),
        compiler_params=_cparams("parallel", "parallel"),
        name="gated_merge",
    )(a, b, wa, wb, gates, gates)


def _ple_gate_kernel(xb_ref, ssq_ref, wg_ref, p_ref, wp_ref, o_ref, rs_ref):
    @pl.when(pl.program_id(1) == 0)
    def _():
        rs_ref[...] = _row_scale(ssq_ref, xb_ref.shape[1])

    g = jnp.dot(xb_ref[...], wg_ref[...], preferred_element_type=F32) * rs_ref[...]
    pp = jnp.dot(p_ref[...].astype(BF16), wp_ref[...], preferred_element_type=F32)
    o_ref[...] = _sigmoid(g) * pp


def ple_gate(xb, ssq, wg, p, wp):
    m, k = xb.shape
    kp = p.shape[1]
    n = wg.shape[1]
    tm, tn = _tile(m, MM_TILE), _tile(n, MM_TILE)
    return pl.pallas_call(
        _ple_gate_kernel,
        grid=(m // tm, n // tn),
        in_specs=[
            pl.BlockSpec((tm, k), lambda i, j: (i, 0)),
            pl.BlockSpec((tm, LANES), lambda i, j: (i, 0)),
            pl.BlockSpec((k, tn), lambda i, j: (0, j)),
            pl.BlockSpec((tm, kp), lambda i, j: (i, 0)),
            pl.BlockSpec((kp, tn), lambda i, j: (0, j)),
        ],
        out_specs=pl.BlockSpec((tm, tn), lambda i, j: (i, j)),
        out_shape=jax.ShapeDtypeStruct((m, n), F32),
        scratch_shapes=[pltpu.VMEM((tm, 1), F32)],
        compiler_params=_cparams("parallel", "arbitrary"),
        name="ple_gate",
    )(xb, ssq, wg, p, wp)


def _final_kernel(x_ref, t_ref, gp_ref, gf_ref, o_ref):
    t = t_ref[...]
    x = x_ref[...] + (t * _rms_scale(t)) * gp_ref[...]
    o_ref[...] = (x * _rms_scale(x)) * gf_ref[...]


def ple_add_final_norm(x, t, post_g, final_g):
    m, d = x.shape
    tm = _tile(m, 256)
    row = pl.BlockSpec((tm, d), lambda i: (i, 0))
    vec = pl.BlockSpec((1, d), lambda i: (0, 0))
    return pl.pallas_call(
        _final_kernel,
        grid=(m // tm,),
        in_specs=[row, row, vec, vec],
        out_specs=row,
        out_shape=jax.ShapeDtypeStruct((m, d), F32),
        compiler_params=_cparams("parallel"),
        name="ple_add_final_norm",
    )(x, t, post_g.reshape(1, d), final_g.reshape(1, d))


def _rope_tables(seq):
    inv = 1.0 / (ROPE_THETA ** (jnp.arange(0, QK_ROPE_DIM, 2, dtype=F32) / QK_ROPE_DIM))
    ang = jnp.arange(seq, dtype=F32)[:, None] * inv[None, :]
    cos, sin = jnp.cos(ang), jnp.sin(ang)
    zero = jnp.zeros((seq, LANES - QK_ROPE_DIM), F32)
    return jnp.concatenate([cos, cos, zero], axis=1), jnp.concatenate([-sin, sin, zero], axis=1)


def _rotate_half_cols(w):
    half = QK_ROPE_DIM // 2
    return jnp.concatenate([w[..., half:], w[..., :half]], axis=-1)


def _prepare_weights(ffn1_norm, ffn1_wg, ffn1_wu, ffn1_wd, mix_norm, w_in, gm_ws, gm_bs, w_out_a, w_uq, w_ukv,
                     w_out_b, w_out, ffn2_norm, ffn2_wg, ffn2_wu, ffn2_wd, ple_gate_norm, w_ple_gate, w_ple_proj,
                     q_rank, kv_rank, gm_width):
    def up(w, gain):
        return (gain[:, None] * w).astype(BF16)

    w_in_t = up(w_in, mix_norm).T

    def in_cols(lo, hi):
        return w_in_t[lo:hi]

    o_q = 2 * gm_width
    o_kv = o_q + q_rank
    o_kr = o_kv + kv_rank
    o_gate = o_kr + QK_ROPE_DIM
    w_kr = in_cols(o_kr, o_gate)
    half = QK_ROPE_DIM // 2
    w_lat = jnp.concatenate([in_cols(o_q, o_kr), w_kr, w_kr[half:], w_kr[:half]], axis=0)

    heads = w_uq.shape[1] // (QK_NOPE_DIM + QK_ROPE_DIM)
    uq = w_uq.reshape(q_rank, heads, QK_NOPE_DIM + QK_ROPE_DIM)
    uq_rope = uq[:, :, QK_NOPE_DIM:]
    w_uq_p = jnp.concatenate([uq[:, :, :QK_NOPE_DIM], uq_rope, _rotate_half_cols(uq_rope)], axis=2)
    ukv = w_ukv.reshape(kv_rank, heads, QK_NOPE_DIM + V_HEAD_DIM)

    gdim = gm_width // gm_ws.shape[0]
    return dict(
        ffn1_up=(up(ffn1_wg, ffn1_norm), up(ffn1_wu, ffn1_norm)),
        late_casts=((ffn1_wd, None), (ffn2_wg, ffn2_norm), (ffn2_wu, ffn2_norm), (ffn2_wd, None)),
        w_uv=in_cols(0, o_q),
        w_lat=w_lat,
        w_gates=in_cols(o_gate, w_in.shape[1]),
        ws=gm_ws.astype(BF16),
        bs_b=jnp.broadcast_to(gm_bs[:, :, None], gm_bs.shape + (gdim,)).astype(F32),
        w_out_a=w_out_a.astype(BF16),
        w_uq_p=w_uq_p.reshape(q_rank, heads * HEAD_PAD).astype(BF16),
        w_k=ukv[:, :, :QK_NOPE_DIM].reshape(kv_rank, heads * QK_NOPE_DIM).astype(BF16),
        w_v=ukv[:, :, QK_NOPE_DIM:].reshape(kv_rank, heads * V_HEAD_DIM).astype(BF16),
        w_out_b=w_out_b.astype(BF16),
        w_out=w_out.astype(BF16),
        w_ple_gate=up(w_ple_gate, ple_gate_norm),
        w_ple_proj=w_ple_proj.astype(BF16),
        heads=heads,
    )


def _trunk(x3d, p3d, w, norms, q_rank, kv_rank, late_weights):
    batch, seq, d = x3d.shape
    x = x3d.reshape(batch * seq, d)
    p = p3d.reshape(batch * seq, p3d.shape[-1])
    heads = w["heads"]

    xb, ssq = cast_sumsq(x)
    a, casted = ffn_up(xb, ssq, *w["ffn1_up"], weight_casts=() if late_weights else w["late_casts"])
    ffn1_wd, ffn2_wg, ffn2_wu, ffn2_wd = late_weights = late_weights or casted
    x, xb, ssq = matmul_residual(a, ffn1_wd, x, 0.5, 512, 512, "ffn_down", emit_norm_inputs=True)

    uv = matmul_normed(xb, ssq, w["w_uv"], F32, MM_TILE, MM_TILE, "in_proj_uv")
    lat = matmul_normed(xb, ssq, w["w_lat"], F32, 512, w["w_lat"].shape[0], "in_proj_lat")
    gates = matmul_normed(xb, ssq, w["w_gates"], F32, MM_TILE, MM_TILE, "in_proj_gates")

    gm = gmlp_gate(uv, norms["gm_v"], w["ws"], w["bs_b"])

    c1, c2 = _rope_tables(seq)
    q = q_proj(lat, norms["q"], w["w_uq_p"], c1, c2, q_rank, seq)
    k, v = kv_proj(lat, norms["kv"], w["w_k"], w["w_v"], c1, c2, q_rank, kv_rank, heads, seq)
    o = attention(q, k, v, batch, seq, heads)

    merged = gated_merge(gm, o, w["w_out_a"], w["w_out_b"], gates)
    x, xb, ssq = matmul_residual(merged, w["w_out"], x, 1.0, MM_TILE, MM_TILE // 2, "out_proj", emit_norm_inputs=True)

    a, _ = ffn_up(xb, ssq, ffn2_wg, ffn2_wu)
    x, xb, ssq = matmul_residual(a, ffn2_wd, x, 0.5, 512, 512, "ffn_down", emit_norm_inputs=True)

    t = ple_gate(xb, ssq, w["w_ple_gate"], p, w["w_ple_proj"])
    y = ple_add_final_norm(x, t, norms["ple_post"], norms["final"])
    return y.reshape(batch, seq, d), late_weights


def kernel(x_prompt, x_sample, p_prompt, p_sample, ffn1_norm, ffn1_wg, ffn1_wu, ffn1_wd, mix_norm, w_in, gm_v_norm, gm_ws, gm_bs, w_out_a, q_norm, w_uq, kv_norm, w_ukv, w_out_b, w_out, ffn2_norm, ffn2_wg, ffn2_wu, ffn2_wd, ple_gate_norm, w_ple_gate, w_ple_proj, ple_post_norm, final_norm):
    depth = ffn1_norm.shape[0]
    q_rank = q_norm.shape[1]
    kv_rank = kv_norm.shape[1]
    gm_width = gm_v_norm.shape[1]
    assert q_rank % kv_rank == 0 and (q_rank + kv_rank) % LANES == 0

    xs = [x_prompt, x_sample]
    for l in range(depth):
        w = _prepare_weights(ffn1_norm[l], ffn1_wg[l], ffn1_wu[l], ffn1_wd[l], mix_norm[l], w_in[l], gm_ws[l],
                             gm_bs[l], w_out_a[l], w_uq[l], w_ukv[l], w_out_b[l], w_out[l], ffn2_norm[l], ffn2_wg[l],
                             ffn2_wu[l], ffn2_wd[l], ple_gate_norm[l], w_ple_gate[l], w_ple_proj[l], q_rank, kv_rank,
                             gm_width)
        norms = dict(gm_v=gm_v_norm[l], q=q_norm[l], kv=kv_norm[l], ple_post=ple_post_norm[l], final=final_norm)
        assert depth == 1
        late_weights = None
        for g, p in enumerate([p_prompt, p_sample]):
            xs[g], late_weights = _trunk(xs[g], p[l], w, norms, q_rank, kv_rank, late_weights)
    return tuple(xs)
```

```python
import functools

import jax
import jax.numpy as jnp
import numpy as np
from jax import lax
from jax.experimental import pallas as pl
from jax.experimental.pallas import tpu as pltpu

F32 = jnp.float32
BF16 = jnp.bfloat16

RMS_EPS = 1e-6
ROPE_THETA = 10000.0
QK_NOPE_DIM = 128
QK_ROPE_DIM = 64
V_HEAD_DIM = 128
LANES = 128
HEAD_PAD = 2 * LANES
QK_LOG2_SCALE = float((QK_NOPE_DIM + QK_ROPE_DIM) ** -0.5 * np.log2(np.e))
MM_TILE = 1024
FFN_UP_TILE = (2048, 256)
FFN_DOWN_TILE = (512, 512)
ROW_PASS_TILE = 256
LATENT_ROW_TILE = 512
ATTN_TILE = 1024
ATTN_UNROLL_CHUNKS = 8
VMEM_LIMIT = 56 * 1024 * 1024


def _cparams(*sem):
    return pltpu.CompilerParams(dimension_semantics=sem, vmem_limit_bytes=VMEM_LIMIT)


def _tile(n, want):
    if n <= want:
        return n
    t = want
    while n % t:
        t //= 2
    return t


def _sigmoid(x):
    return 1.0 / (1.0 + jnp.exp(-x))


def _gelu_tanh(x):
    c = np.float32(np.sqrt(2.0 / np.pi))
    return x * (0.5 * (1.0 + jnp.tanh(c * (x + 0.044715 * (x * x * x)))))


def _rms_scale(x):
    return lax.rsqrt(jnp.mean(x * x, axis=-1, keepdims=True) + RMS_EPS)


def _lane_partial_sumsq(x):
    x2 = x * x
    return sum(x2[:, c * LANES:(c + 1) * LANES] for c in range(x.shape[1] // LANES))


def _row_scale(ssq_ref, width):
    return lax.rsqrt(jnp.sum(ssq_ref[...], axis=-1, keepdims=True) / width + RMS_EPS)


def _cast_sumsq_kernel(x_ref, xb_ref, ssq_ref):
    x = x_ref[...]
    xb_ref[...] = x.astype(xb_ref.dtype)
    ssq_ref[...] = _lane_partial_sumsq(x)


def cast_sumsq(x):
    m, d = x.shape
    tm = _tile(m, ROW_PASS_TILE)
    return pl.pallas_call(
        _cast_sumsq_kernel,
        grid=(m // tm,),
        in_specs=[pl.BlockSpec((tm, d), lambda i: (i, 0))],
        out_specs=[pl.BlockSpec((tm, d), lambda i: (i, 0)), pl.BlockSpec((tm, LANES), lambda i: (i, 0))],
        out_shape=[jax.ShapeDtypeStruct((m, d), BF16), jax.ShapeDtypeStruct((m, LANES), F32)],
        compiler_params=_cparams("parallel"),
        name="cast_sumsq",
    )(x)


def _mm_normed_kernel(xb_ref, ssq_ref, w_ref, o_ref, rs_ref):
    @pl.when(pl.program_id(1) == 0)
    def _():
        rs_ref[...] = _row_scale(ssq_ref, xb_ref.shape[1])

    acc = lax.dot_general(xb_ref[...], w_ref[...], (((1,), (1,)), ((), ())), preferred_element_type=F32)
    o_ref[...] = (acc * rs_ref[...]).astype(o_ref.dtype)


def matmul_normed(xb, ssq, w_t, out_dtype, tm, tn, name):
    m, k = xb.shape
    n = w_t.shape[0]
    tm, tn = _tile(m, tm), _tile(n, tn)
    return pl.pallas_call(
        _mm_normed_kernel,
        grid=(m // tm, n // tn),
        in_specs=[
            pl.BlockSpec((tm, k), lambda i, j: (i, 0)),
            pl.BlockSpec((tm, LANES), lambda i, j: (i, 0)),
            pl.BlockSpec((tn, k), lambda i, j: (j, 0)),
        ],
        out_specs=pl.BlockSpec((tm, tn), lambda i, j: (i, j)),
        out_shape=jax.ShapeDtypeStruct((m, n), out_dtype),
        scratch_shapes=[pltpu.VMEM((tm, 1), F32)],
        compiler_params=_cparams("parallel", "arbitrary"),
        name=name,
    )(xb, ssq, w_t)


def _mm_residual_kernel(a_ref, w_ref, x_ref, o_ref, *norm_refs, scale):
    acc = jnp.dot(a_ref[...], w_ref[...], preferred_element_type=F32)
    y = x_ref[...] + (acc if scale == 1.0 else scale * acc)
    o_ref[...] = y
    if norm_refs:
        yb_ref, ssq_ref = norm_refs
        yb_ref[...] = y.astype(yb_ref.dtype)

        @pl.when(pl.program_id(1) == 0)
        def _():
            ssq_ref[...] = jnp.zeros_like(ssq_ref)

        ssq_ref[...] += _lane_partial_sumsq(y)


def matmul_residual(a, w, x, scale, tm, tn, name, emit_norm_inputs=False):
    m, k = a.shape
    n = w.shape[1]
    tm, tn = _tile(m, tm), _tile(n, tn)
    tile = pl.BlockSpec((tm, tn), lambda i, j: (i, j))
    out_specs, out_shape = [tile], [jax.ShapeDtypeStruct((m, n), F32)]
    if emit_norm_inputs:
        out_specs += [tile, pl.BlockSpec((tm, LANES), lambda i, j: (i, 0))]
        out_shape += [jax.ShapeDtypeStruct((m, n), BF16), jax.ShapeDtypeStruct((m, LANES), F32)]
    out = pl.pallas_call(
        functools.partial(_mm_residual_kernel, scale=scale),
        grid=(m // tm, n // tn),
        in_specs=[pl.BlockSpec((tm, k), lambda i, j: (i, 0)), pl.BlockSpec((k, tn), lambda i, j: (0, j)), tile],
        out_specs=out_specs,
        out_shape=out_shape,
        compiler_params=_cparams("parallel", "arbitrary"),
        name=name,
    )(a, w, x)
    return out if emit_norm_inputs else out[0]


def _ffn_up_kernel(xb_ref, ssq_ref, wg_ref, wu_ref, *refs, row_splits, cast_has_gain):
    n_casts = len(cast_has_gain)
    n_cast_in = n_casts + sum(cast_has_gain)
    cast_in, o_ref, cast_out, rs_ref = refs[:n_cast_in], refs[n_cast_in], refs[n_cast_in + 1:-1], refs[-1]

    @pl.when(pl.program_id(1) == 0)
    def _():
        rs_ref[...] = _row_scale(ssq_ref, xb_ref.shape[1])

    rows = xb_ref.shape[0] // row_splits
    for r in range(row_splits):
        sl = slice(r * rows, (r + 1) * rows)
        xb = xb_ref[sl, :]
        rs = rs_ref[sl, :]
        g = jnp.dot(xb, wg_ref[...], preferred_element_type=F32) * rs
        u = jnp.dot(xb, wu_ref[...], preferred_element_type=F32) * rs
        o_ref[sl, :] = ((g * _sigmoid(g)) * u).astype(o_ref.dtype)

    pos = 0
    for c, has_gain in enumerate(cast_has_gain):
        w = cast_in[pos][...]
        if has_gain:
            w = cast_in[pos + 1][...] * w
        cast_out[c][...] = w.astype(cast_out[c].dtype)
        pos += 1 + has_gain


def ffn_up(xb, ssq, wg, wu, weight_casts=()):
    m, k = xb.shape
    n = wg.shape[1]
    tm, tn = _tile(m, FFN_UP_TILE[0]), _tile(n, FFN_UP_TILE[1])
    ni, nj = m // tm, n // tn
    in_specs = [
        pl.BlockSpec((tm, k), lambda i, j: (i, 0)),
        pl.BlockSpec((tm, LANES), lambda i, j: (i, 0)),
        pl.BlockSpec((k, tn), lambda i, j: (0, j)),
        pl.BlockSpec((k, tn), lambda i, j: (0, j)),
    ]
    out_specs = [pl.BlockSpec((tm, tn), lambda i, j: (i, j))]
    out_shape = [jax.ShapeDtypeStruct((m, n), BF16)]
    cast_args = []
    for w, gain in weight_casts:
        r, c = w.shape
        if r % ni == 0 and c % nj == 0:
            blk, idx, gidx = (r // ni, c // nj), (lambda i, j: (i, j)), (lambda i, j: (i, 0))
        else:
            blk, idx, gidx = (r // nj, c // ni), (lambda i, j: (j, i)), (lambda i, j: (j, 0))
        assert blk[0] * (r // blk[0]) == r and blk[1] * (c // blk[1]) == c and (r // blk[0]) * (c // blk[1]) == ni * nj
        in_specs.append(pl.BlockSpec(blk, idx))
        cast_args.append(w)
        if gain is not None:
            in_specs.append(pl.BlockSpec((blk[0], 1), gidx))
            cast_args.append(gain.reshape(r, 1))
        out_specs.append(pl.BlockSpec(blk, idx))
        out_shape.append(jax.ShapeDtypeStruct((r, c), BF16))
    out = pl.pallas_call(
        functools.partial(_ffn_up_kernel, row_splits=4 if tm % 1024 == 0 else 1,
                          cast_has_gain=tuple(g is not None for _, g in weight_casts)),
        grid=(ni, nj),
        in_specs=in_specs,
        out_specs=out_specs,
        out_shape=out_shape,
        scratch_shapes=[pltpu.VMEM((tm, 1), F32)],
        compiler_params=_cparams("parallel", "arbitrary"),
        name="ffn_up",
    )(xb, ssq, wg, wu, *cast_args)
    return out[0], out[1:]


def _gmlp_kernel(u_ref, v_ref, vg_ref, ws_ref, bs_ref, o_ref, vn_ref, *, chunk, groups, gdim):
    v = _gelu_tanh(v_ref[...])
    vn_ref[...] = ((v * _rms_scale(v)) * vg_ref[...]).astype(vn_ref.dtype)
    tm = u_ref.shape[0]
    for c in range(tm // chunk):
        rows = slice(c * chunk, (c + 1) * chunk)
        for g in range(groups):
            cols = slice(g * gdim, (g + 1) * gdim)
            mixed = jnp.dot(ws_ref[g], vn_ref[rows, cols], preferred_element_type=F32) + bs_ref[g]
            o_ref[rows, cols] = (_gelu_tanh(u_ref[rows, cols]) * mixed).astype(o_ref.dtype)


def gmlp_gate(uv, v_norm, ws, bs_b):
    m = uv.shape[0]
    groups, chunk, _ = ws.shape
    gdim = bs_b.shape[2]
    w = groups * gdim
    tm = _tile(m, 2 * chunk)
    return pl.pallas_call(
        functools.partial(_gmlp_kernel, chunk=chunk, groups=groups, gdim=gdim),
        grid=(m // tm,),
        in_specs=[
            pl.BlockSpec((tm, w), lambda i: (i, 0)),
            pl.BlockSpec((tm, w), lambda i: (i, 1)),
            pl.BlockSpec((1, w), lambda i: (0, 0)),
            pl.BlockSpec((groups, chunk, chunk), lambda i: (0, 0, 0)),
            pl.BlockSpec((groups, chunk, gdim), lambda i: (0, 0, 0)),
        ],
        out_specs=pl.BlockSpec((tm, w), lambda i: (i, 0)),
        out_shape=jax.ShapeDtypeStruct((m, w), BF16),
        scratch_shapes=[pltpu.VMEM((tm, w), BF16)],
        compiler_params=_cparams("parallel"),
        name="gmlp_gate",
    )(uv, uv, v_norm.reshape(1, w), ws, bs_b)


def _rope_block(blk, c1, c2):
    return blk * c1 + pltpu.roll(blk, LANES // 2, 1) * c2


def _q_proj_kernel(ql_ref, g_ref, w_ref, c1_ref, c2_ref, o_ref, hn_ref, *, heads_per_tile):
    @pl.when(pl.program_id(1) == 0)
    def _():
        x = ql_ref[...]
        hn_ref[...] = ((x * _rms_scale(x)) * g_ref[...]).astype(hn_ref.dtype)

    acc = jnp.dot(hn_ref[...], w_ref[...], preferred_element_type=F32)
    c1, c2 = c1_ref[...], c2_ref[...]
    for h in range(heads_per_tile):
        lo = h * HEAD_PAD
        o_ref[:, lo:lo + LANES] = (acc[:, lo:lo + LANES] * QK_LOG2_SCALE).astype(o_ref.dtype)
        roped = _rope_block(acc[:, lo + LANES:lo + HEAD_PAD], c1, c2)
        o_ref[:, lo + LANES:lo + HEAD_PAD] = (roped * QK_LOG2_SCALE).astype(o_ref.dtype)


def q_proj(lat, q_norm, w_uq_p, c1, c2, q_rank, seq):
    m = lat.shape[0]
    n = w_uq_p.shape[1]
    tm = _tile(seq, MM_TILE)
    tn = n
    pos_blocks = seq // tm
    return pl.pallas_call(
        functools.partial(_q_proj_kernel, heads_per_tile=tn // HEAD_PAD),
        grid=(m // tm, n // tn),
        in_specs=[
            pl.BlockSpec((tm, q_rank), lambda i, j: (i, 0)),
            pl.BlockSpec((1, q_rank), lambda i, j: (0, 0)),
            pl.BlockSpec((q_rank, tn), lambda i, j: (0, j)),
            pl.BlockSpec((tm, LANES), lambda i, j: (i % pos_blocks, 0)),
            pl.BlockSpec((tm, LANES), lambda i, j: (i % pos_blocks, 0)),
        ],
        out_specs=pl.BlockSpec((tm, tn), lambda i, j: (i, j)),
        out_shape=jax.ShapeDtypeStruct((m, n), BF16),
        scratch_shapes=[pltpu.VMEM((tm, q_rank), BF16)],
        compiler_params=_cparams("parallel", "arbitrary"),
        name="q_proj",
    )(lat, q_norm.reshape(1, q_rank), w_uq_p, c1, c2)


def _kv_proj_kernel(kvl_ref, kr_ref, g_ref, wk_ref, wv_ref, c1_ref, c2_ref, k_ref, v_ref, *, heads):
    x = kvl_ref[...]
    hn = ((x * _rms_scale(x)) * g_ref[...]).astype(BF16)
    kn = jnp.dot(hn, wk_ref[...], preferred_element_type=F32)
    v_ref[...] = jnp.dot(hn, wv_ref[...], preferred_element_type=F32).astype(v_ref.dtype)
    kr = _rope_block(kr_ref[...], c1_ref[...], c2_ref[...]).astype(k_ref.dtype)
    for h in range(heads):
        lo = h * HEAD_PAD
        k_ref[:, lo:lo + LANES] = kn[:, h * QK_NOPE_DIM:(h + 1) * QK_NOPE_DIM].astype(k_ref.dtype)
        k_ref[:, lo + LANES:lo + HEAD_PAD] = kr


def kv_proj(lat, kv_norm, w_k, w_v, c1, c2, q_rank, kv_rank, heads, seq):
    m = lat.shape[0]
    tm = _tile(seq, LATENT_ROW_TILE)
    pos_blocks = seq // tm
    kv_blk = q_rank // kv_rank
    kr_blk = (q_rank + kv_rank) // LANES
    return pl.pallas_call(
        functools.partial(_kv_proj_kernel, heads=heads),
        grid=(m // tm,),
        in_specs=[
            pl.BlockSpec((tm, kv_rank), lambda i: (i, kv_blk)),
            pl.BlockSpec((tm, LANES), lambda i: (i, kr_blk)),
            pl.BlockSpec((1, kv_rank), lambda i: (0, 0)),
            pl.BlockSpec(w_k.shape, lambda i: (0, 0)),
            pl.BlockSpec(w_v.shape, lambda i: (0, 0)),
            pl.BlockSpec((tm, LANES), lambda i: (i % pos_blocks, 0)),
            pl.BlockSpec((tm, LANES), lambda i: (i % pos_blocks, 0)),
        ],
        out_specs=[
            pl.BlockSpec((tm, heads * HEAD_PAD), lambda i: (i, 0)),
            pl.BlockSpec((tm, heads * V_HEAD_DIM), lambda i: (i, 0)),
        ],
        out_shape=[
            jax.ShapeDtypeStruct((m, heads * HEAD_PAD), BF16),
            jax.ShapeDtypeStruct((m, heads * V_HEAD_DIM), BF16),
        ],
        compiler_params=_cparams("parallel"),
        name="kv_proj",
    )(lat, lat, kv_norm.reshape(1, kv_rank), w_k, w_v, c1, c2)


def _attn_kernel(q_ref, k_ref, v_ref, o_ref, *, tq, tk):
    seq = k_ref.shape[0]
    nq, nk = seq // tq, seq // tk
    kv_unroll = min(nk, ATTN_UNROLL_CHUNKS)
    q_unroll = max(1, min(nq, ATTN_UNROLL_CHUNKS // kv_unroll))

    def q_body(qi, _):
        q0 = pl.multiple_of(qi * tq, tq)
        q = q_ref[pl.ds(q0, tq), :]

        def kv_body(ki, carry):
            m, l, acc = carry
            k0 = pl.multiple_of(ki * tk, tk)
            k = k_ref[pl.ds(k0, tk), :]
            s = lax.dot_general(q, k, (((1,), (1,)), ((), ())), preferred_element_type=F32)
            m_new = jnp.maximum(m, jnp.max(s, axis=-1, keepdims=True))
            p = jnp.exp2(s - m_new)
            alpha = jnp.exp2(m - m_new)
            l = alpha * l + jnp.sum(p, axis=-1, keepdims=True)
            pv = jnp.dot(p.astype(BF16), v_ref[pl.ds(k0, tk), :], preferred_element_type=F32)
            return m_new, l, alpha * acc + pv

        init = (jnp.full((tq, 1), -jnp.inf, F32), jnp.zeros((tq, 1), F32), jnp.zeros((tq, V_HEAD_DIM), F32))
        _, l, acc = lax.fori_loop(0, nk, kv_body, init, unroll=kv_unroll)
        o_ref[pl.ds(q0, tq), :] = (acc / l).astype(o_ref.dtype)
        return 0

    lax.fori_loop(0, nq, q_body, 0, unroll=q_unroll)


def attention(q, k, v, batch, seq, heads):
    m = q.shape[0]
    tq, tk = _tile(seq, ATTN_TILE), _tile(seq, ATTN_TILE)
    return pl.pallas_call(
        functools.partial(_attn_kernel, tq=tq, tk=tk),
        grid=(batch, heads),
        in_specs=[
            pl.BlockSpec((seq, HEAD_PAD), lambda b, h: (b, h)),
            pl.BlockSpec((seq, HEAD_PAD), lambda b, h: (b, h)),
            pl.BlockSpec((seq, V_HEAD_DIM), lambda b, h: (b, h)),
        ],
        out_specs=pl.BlockSpec((seq, V_HEAD_DIM), lambda b, h: (b, h)),
        out_shape=jax.ShapeDtypeStruct((m, heads * V_HEAD_DIM), BF16),
        compiler_params=_cparams("parallel", "parallel"),
        name="attention",
    )(q, k, v)


def _merge_kernel(a_ref, b_ref, wa_ref, wb_ref, ga_ref, gb_ref, o_ref):
    ya = jnp.dot(a_ref[...], wa_ref[...], preferred_element_type=F32)
    yb = jnp.dot(b_ref[...], wb_ref[...], preferred_element_type=F32)
    o_ref[...] = (_sigmoid(ga_ref[...]) * ya + _sigmoid(gb_ref[...]) * yb).astype(o_ref.dtype)


def gated_merge(a, b, wa, wb, gates):
    m, ka = a.shape
    kb = b.shape[1]
    n = wa.shape[1]
    tm, tn = _tile(m, MM_TILE), _tile(n, MM_TILE // 2)
    nj = n // tn
    return pl.pallas_call(
        _merge_kernel,
        grid=(m // tm, nj),
        in_specs=[
            pl.BlockSpec((tm, ka), lambda i, j: (i, 0)),
            pl.BlockSpec((tm, kb), lambda i, j: (i, 0)),
            pl.BlockSpec((ka, tn), lambda i, j: (0, j)),
            pl.BlockSpec((kb, tn), lambda i, j: (0, j)),
            pl.BlockSpec((tm, tn), lambda i, j: (i, j)),
            pl.BlockSpec((tm, tn), lambda i, j: (i, j + nj)),
        ],
        out_specs=pl.BlockSpec((tm, tn), lambda i, j: (i, j)),
        out_shape=jax.ShapeDtypeStruct((m, n), BF16),
        compiler_params=_cparams("parallel", "parallel"),
        name="gated_merge",
    )(a, b, wa, wb, gates, gates)


def _ple_gate_kernel(xb_ref, ssq_ref, wg_ref, p_ref, wp_ref, o_ref, rs_ref):
    @pl.when(pl.program_id(1) == 0)
    def _():
        rs_ref[...] = _row_scale(ssq_ref, xb_ref.shape[1])

    g = jnp.dot(xb_ref[...], wg_ref[...], preferred_element_type=F32) * rs_ref[...]
    pp = jnp.dot(p_ref[...].astype(BF16), wp_ref[...], preferred_element_type=F32)
    o_ref[...] = _sigmoid(g) * pp


def ple_gate(xb, ssq, wg, p, wp):
    m, k = xb.shape
    kp = p.shape[1]
    n = wg.shape[1]
    tm, tn = _tile(m, MM_TILE), _tile(n, MM_TILE)
    return pl.pallas_call(
        _ple_gate_kernel,
        grid=(m // tm, n // tn),
        in_specs=[
            pl.BlockSpec((tm, k), lambda i, j: (i, 0)),
            pl.BlockSpec((tm, LANES), lambda i, j: (i, 0)),
            pl.BlockSpec((k, tn), lambda i, j: (0, j)),
            pl.BlockSpec((tm, kp), lambda i, j: (i, 0)),
            pl.BlockSpec((kp, tn), lambda i, j: (0, j)),
        ],
        out_specs=pl.BlockSpec((tm, tn), lambda i, j: (i, j)),
        out_shape=jax.ShapeDtypeStruct((m, n), F32),
        scratch_shapes=[pltpu.VMEM((tm, 1), F32)],
        compiler_params=_cparams("parallel", "arbitrary"),
        name="ple_gate",
    )(xb, ssq, wg, p, wp)


def _final_kernel(x_ref, t_ref, gp_ref, gf_ref, o_ref):
    t = t_ref[...]
    x = x_ref[...] + (t * _rms_scale(t)) * gp_ref[...]
    o_ref[...] = (x * _rms_scale(x)) * gf_ref[...]


def ple_add_final_norm(x, t, post_g, final_g):
    m, d = x.shape
    tm = _tile(m, ROW_PASS_TILE)
    row = pl.BlockSpec((tm, d), lambda i: (i, 0))
    vec = pl.BlockSpec((1, d), lambda i: (0, 0))
    return pl.pallas_call(
        _final_kernel,
        grid=(m // tm,),
        in_specs=[row, row, vec, vec],
        out_specs=row,
        out_shape=jax.ShapeDtypeStruct((m, d), F32),
        compiler_params=_cparams("parallel"),
        name="ple_add_final_norm",
    )(x, t, post_g.reshape(1, d), final_g.reshape(1, d))


def _rope_tables(seq):
    inv = 1.0 / (ROPE_THETA ** (jnp.arange(0, QK_ROPE_DIM, 2, dtype=F32) / QK_ROPE_DIM))
    ang = jnp.arange(seq, dtype=F32)[:, None] * inv[None, :]
    cos, sin = jnp.cos(ang), jnp.sin(ang)
    zero = jnp.zeros((seq, LANES - QK_ROPE_DIM), F32)
    return jnp.concatenate([cos, cos, zero], axis=1), jnp.concatenate([-sin, sin, zero], axis=1)


def _rotate_half_cols(w):
    half = QK_ROPE_DIM // 2
    return jnp.concatenate([w[..., half:], w[..., :half]], axis=-1)


def _prepare_weights(ffn1_norm, ffn1_wg, ffn1_wu, ffn1_wd, mix_norm, w_in, gm_ws, gm_bs, w_out_a, w_uq, w_ukv,
                     w_out_b, w_out, ffn2_norm, ffn2_wg, ffn2_wu, ffn2_wd, ple_gate_norm, w_ple_gate, w_ple_proj,
                     q_rank, kv_rank, gm_width):
    def up(w, gain):
        return (gain[:, None] * w).astype(BF16)

    w_in_t = up(w_in, mix_norm).T

    def in_cols(lo, hi):
        return w_in_t[lo:hi]

    o_q = 2 * gm_width
    o_kv = o_q + q_rank
    o_kr = o_kv + kv_rank
    o_gate = o_kr + QK_ROPE_DIM
    w_kr = in_cols(o_kr, o_gate)
    half = QK_ROPE_DIM // 2
    w_lat = jnp.concatenate([in_cols(o_q, o_kr), w_kr, w_kr[half:], w_kr[:half]], axis=0)

    heads = w_uq.shape[1] // (QK_NOPE_DIM + QK_ROPE_DIM)
    uq = w_uq.reshape(q_rank, heads, QK_NOPE_DIM + QK_ROPE_DIM)
    uq_rope = uq[:, :, QK_NOPE_DIM:]
    w_uq_p = jnp.concatenate([uq[:, :, :QK_NOPE_DIM], uq_rope, _rotate_half_cols(uq_rope)], axis=2)
    ukv = w_ukv.reshape(kv_rank, heads, QK_NOPE_DIM + V_HEAD_DIM)

    gdim = gm_width // gm_ws.shape[0]
    return dict(
        ffn1_up=(up(ffn1_wg, ffn1_norm), up(ffn1_wu, ffn1_norm)),
        late_casts=((ffn1_wd, None), (ffn2_wg, ffn2_norm), (ffn2_wu, ffn2_norm), (ffn2_wd, None)),
        w_uv=in_cols(0, o_q),
        w_lat=w_lat,
        w_gates=in_cols(o_gate, w_in.shape[1]),
        ws=gm_ws.astype(BF16),
        bs_b=jnp.broadcast_to(gm_bs[:, :, None], gm_bs.shape + (gdim,)).astype(F32),
        w_out_a=w_out_a.astype(BF16),
        w_uq_p=w_uq_p.reshape(q_rank, heads * HEAD_PAD).astype(BF16),
        w_k=ukv[:, :, :QK_NOPE_DIM].reshape(kv_rank, heads * QK_NOPE_DIM).astype(BF16),
        w_v=ukv[:, :, QK_NOPE_DIM:].reshape(kv_rank, heads * V_HEAD_DIM).astype(BF16),
        w_out_b=w_out_b.astype(BF16),
        w_out=w_out.astype(BF16),
        w_ple_gate=up(w_ple_gate, ple_gate_norm),
        w_ple_proj=w_ple_proj.astype(BF16),
        heads=heads,
    )


def _trunk(x3d, p3d, w, norms, q_rank, kv_rank, late_weights):
    batch, seq, d = x3d.shape
    x = x3d.reshape(batch * seq, d)
    p = p3d.reshape(batch * seq, p3d.shape[-1])
    heads = w["heads"]

    xb, ssq = cast_sumsq(x)
    a, casted = ffn_up(xb, ssq, *w["ffn1_up"], weight_casts=() if late_weights else w["late_casts"])
    ffn1_wd, ffn2_wg, ffn2_wu, ffn2_wd = late_weights = late_weights or casted
    x, xb, ssq = matmul_residual(a, ffn1_wd, x, 0.5, *FFN_DOWN_TILE, "ffn_down", emit_norm_inputs=True)

    uv = matmul_normed(xb, ssq, w["w_uv"], F32, MM_TILE, MM_TILE, "in_proj_uv")
    lat = matmul_normed(xb, ssq, w["w_lat"], F32, LATENT_ROW_TILE, w["w_lat"].shape[0], "in_proj_lat")
    gates = matmul_normed(xb, ssq, w["w_gates"], F32, MM_TILE, MM_TILE, "in_proj_gates")

    gm = gmlp_gate(uv, norms["gm_v"], w["ws"], w["bs_b"])

    c1, c2 = _rope_tables(seq)
    q = q_proj(lat, norms["q"], w["w_uq_p"], c1, c2, q_rank, seq)
    k, v = kv_proj(lat, norms["kv"], w["w_k"], w["w_v"], c1, c2, q_rank, kv_rank, heads, seq)
    o = attention(q, k, v, batch, seq, heads)

    merged = gated_merge(gm, o, w["w_out_a"], w["w_out_b"], gates)
    x, xb, ssq = matmul_residual(merged, w["w_out"], x, 1.0, MM_TILE, MM_TILE // 2, "out_proj", emit_norm_inputs=True)

    a, _ = ffn_up(xb, ssq, ffn2_wg, ffn2_wu)
    x, xb, ssq = matmul_residual(a, ffn2_wd, x, 0.5, *FFN_DOWN_TILE, "ffn_down", emit_norm_inputs=True)

    t = ple_gate(xb, ssq, w["w_ple_gate"], p, w["w_ple_proj"])
    y = ple_add_final_norm(x, t, norms["ple_post"], norms["final"])
    return y.reshape(batch, seq, d), late_weights


def kernel(x_prompt, x_sample, p_prompt, p_sample, ffn1_norm, ffn1_wg, ffn1_wu, ffn1_wd, mix_norm, w_in, gm_v_norm, gm_ws, gm_bs, w_out_a, q_norm, w_uq, kv_norm, w_ukv, w_out_b, w_out, ffn2_norm, ffn2_wg, ffn2_wu, ffn2_wd, ple_gate_norm, w_ple_gate, w_ple_proj, ple_post_norm, final_norm):
    depth = ffn1_norm.shape[0]
    q_rank = q_norm.shape[1]
    kv_rank = kv_norm.shape[1]
    gm_width = gm_v_norm.shape[1]
    assert q_rank % kv_rank == 0 and (q_rank + kv_rank) % LANES == 0

    xs = [x_prompt, x_sample]
    for l in range(depth):
        w = _prepare_weights(ffn1_norm[l], ffn1_wg[l], ffn1_wu[l], ffn1_wd[l], mix_norm[l], w_in[l], gm_ws[l],
                             gm_bs[l], w_out_a[l], w_uq[l], w_ukv[l], w_out_b[l], w_out[l], ffn2_norm[l], ffn2_wg[l],
                             ffn2_wu[l], ffn2_wd[l], ple_gate_norm[l], w_ple_gate[l], w_ple_proj[l], q_rank, kv_rank,
                             gm_width)
        norms = dict(gm_v=gm_v_norm[l], q=q_norm[l], kv=kv_norm[l], ple_post=ple_post_norm[l], final=final_norm)
        assert depth == 1
        late_weights = None
        for g, p in enumerate([p_prompt, p_sample]):
            xs[g], late_weights = _trunk(xs[g], p[l], w, norms, q_rank, kv_rank, late_weights)
    return tuple(xs)
```

```python
import functools

import jax
import jax.numpy as jnp
import numpy as np
from jax import lax
from jax.experimental import pallas as pl
from jax.experimental.pallas import tpu as pltpu

F32 = jnp.float32
BF16 = jnp.bfloat16

RMS_EPS = 1e-6
ROPE_THETA = 10000.0
QK_NOPE_DIM = 128
QK_ROPE_DIM = 64
V_HEAD_DIM = 128
LANES = 128
HEAD_PAD = 2 * LANES
QK_LOG2_SCALE = float((QK_NOPE_DIM + QK_ROPE_DIM) ** -0.5 * np.log2(np.e))
MM_TILE = 1024
FFN_UP_TILE = (2048, 256)
FFN_DOWN_TILE = (512, 512)
ROW_PASS_TILE = 512
NORM_PASS_TILE = 256
LATENT_ROW_TILE = 512
ATTN_TILE = 1024
ATTN_UNROLL_CHUNKS = 8
VMEM_LIMIT = 56 * 1024 * 1024


def _cparams(*sem):
    return pltpu.CompilerParams(dimension_semantics=sem, vmem_limit_bytes=VMEM_LIMIT)


def _tile(n, want):
    if n <= want:
        return n
    t = want
    while n % t:
        t //= 2
    return t


def _sigmoid(x):
    return 1.0 / (1.0 + jnp.exp(-x))


def _gelu_tanh(x):
    c = np.float32(np.sqrt(2.0 / np.pi))
    return x * (0.5 * (1.0 + jnp.tanh(c * (x + 0.044715 * (x * x * x)))))


def _rms_scale(x):
    return lax.rsqrt(jnp.mean(x * x, axis=-1, keepdims=True) + RMS_EPS)


def _lane_partial_sumsq(x):
    x2 = x * x
    return sum(x2[:, c * LANES:(c + 1) * LANES] for c in range(x.shape[1] // LANES))


def _row_scale(ssq_ref, width):
    return lax.rsqrt(jnp.sum(ssq_ref[...], axis=-1, keepdims=True) / width + RMS_EPS)


def _cast_sumsq_kernel(x_ref, xb_ref, ssq_ref):
    x = x_ref[...]
    xb_ref[...] = x.astype(xb_ref.dtype)
    ssq_ref[...] = _lane_partial_sumsq(x)


def cast_sumsq(x):
    m, d = x.shape
    tm = _tile(m, ROW_PASS_TILE)
    return pl.pallas_call(
        _cast_sumsq_kernel,
        grid=(m // tm,),
        in_specs=[pl.BlockSpec((tm, d), lambda i: (i, 0))],
        out_specs=[pl.BlockSpec((tm, d), lambda i: (i, 0)), pl.BlockSpec((tm, LANES), lambda i: (i, 0))],
        out_shape=[jax.ShapeDtypeStruct((m, d), BF16), jax.ShapeDtypeStruct((m, LANES), F32)],
        compiler_params=_cparams("parallel"),
        name="cast_sumsq",
    )(x)


def _mm_normed_kernel(xb_ref, ssq_ref, w_ref, o_ref, rs_ref):
    @pl.when(pl.program_id(1) == 0)
    def _():
        rs_ref[...] = _row_scale(ssq_ref, xb_ref.shape[1])

    acc = lax.dot_general(xb_ref[...], w_ref[...], (((1,), (1,)), ((), ())), preferred_element_type=F32)
    o_ref[...] = (acc * rs_ref[...]).astype(o_ref.dtype)


def matmul_normed(xb, ssq, w_t, out_dtype, tm, tn, name):
    m, k = xb.shape
    n = w_t.shape[0]
    tm, tn = _tile(m, tm), _tile(n, tn)
    return pl.pallas_call(
        _mm_normed_kernel,
        grid=(m // tm, n // tn),
        in_specs=[
            pl.BlockSpec((tm, k), lambda i, j: (i, 0)),
            pl.BlockSpec((tm, LANES), lambda i, j: (i, 0)),
            pl.BlockSpec((tn, k), lambda i, j: (j, 0)),
        ],
        out_specs=pl.BlockSpec((tm, tn), lambda i, j: (i, j)),
        out_shape=jax.ShapeDtypeStruct((m, n), out_dtype),
        scratch_shapes=[pltpu.VMEM((tm, 1), F32)],
        compiler_params=_cparams("parallel", "arbitrary"),
        name=name,
    )(xb, ssq, w_t)


def _mm_residual_kernel(a_ref, w_ref, x_ref, o_ref, *norm_refs, scale):
    acc = jnp.dot(a_ref[...], w_ref[...], preferred_element_type=F32)
    y = x_ref[...] + (acc if scale == 1.0 else scale * acc)
    o_ref[...] = y
    if norm_refs:
        yb_ref, ssq_ref = norm_refs
        yb_ref[...] = y.astype(yb_ref.dtype)

        @pl.when(pl.program_id(1) == 0)
        def _():
            ssq_ref[...] = jnp.zeros_like(ssq_ref)

        ssq_ref[...] += _lane_partial_sumsq(y)


def matmul_residual(a, w, x, scale, tm, tn, name, emit_norm_inputs=False):
    m, k = a.shape
    n = w.shape[1]
    tm, tn = _tile(m, tm), _tile(n, tn)
    tile = pl.BlockSpec((tm, tn), lambda i, j: (i, j))
    out_specs, out_shape = [tile], [jax.ShapeDtypeStruct((m, n), F32)]
    if emit_norm_inputs:
        out_specs += [tile, pl.BlockSpec((tm, LANES), lambda i, j: (i, 0))]
        out_shape += [jax.ShapeDtypeStruct((m, n), BF16), jax.ShapeDtypeStruct((m, LANES), F32)]
    out = pl.pallas_call(
        functools.partial(_mm_residual_kernel, scale=scale),
        grid=(m // tm, n // tn),
        in_specs=[pl.BlockSpec((tm, k), lambda i, j: (i, 0)), pl.BlockSpec((k, tn), lambda i, j: (0, j)), tile],
        out_specs=out_specs,
        out_shape=out_shape,
        compiler_params=_cparams("parallel", "arbitrary"),
        name=name,
    )(a, w, x)
    return out if emit_norm_inputs else out[0]


def _ffn_up_kernel(xb_ref, ssq_ref, wg_ref, wu_ref, *refs, row_splits, cast_has_gain):
    n_casts = len(cast_has_gain)
    n_cast_in = n_casts + sum(cast_has_gain)
    cast_in, o_ref, cast_out, rs_ref = refs[:n_cast_in], refs[n_cast_in], refs[n_cast_in + 1:-1], refs[-1]

    @pl.when(pl.program_id(1) == 0)
    def _():
        rs_ref[...] = _row_scale(ssq_ref, xb_ref.shape[1])

    rows = xb_ref.shape[0] // row_splits
    for r in range(row_splits):
        sl = slice(r * rows, (r + 1) * rows)
        xb = xb_ref[sl, :]
        rs = rs_ref[sl, :]
        g = jnp.dot(xb, wg_ref[...], preferred_element_type=F32) * rs
        u = jnp.dot(xb, wu_ref[...], preferred_element_type=F32) * rs
        o_ref[sl, :] = ((g * _sigmoid(g)) * u).astype(o_ref.dtype)

    pos = 0
    for c, has_gain in enumerate(cast_has_gain):
        w = cast_in[pos][...]
        if has_gain:
            w = cast_in[pos + 1][...] * w
        cast_out[c][...] = w.astype(cast_out[c].dtype)
        pos += 1 + has_gain


def ffn_up(xb, ssq, wg, wu, weight_casts=()):
    m, k = xb.shape
    n = wg.shape[1]
    tm, tn = _tile(m, FFN_UP_TILE[0]), _tile(n, FFN_UP_TILE[1])
    ni, nj = m // tm, n // tn
    in_specs = [
        pl.BlockSpec((tm, k), lambda i, j: (i, 0)),
        pl.BlockSpec((tm, LANES), lambda i, j: (i, 0)),
        pl.BlockSpec((k, tn), lambda i, j: (0, j)),
        pl.BlockSpec((k, tn), lambda i, j: (0, j)),
    ]
    out_specs = [pl.BlockSpec((tm, tn), lambda i, j: (i, j))]
    out_shape = [jax.ShapeDtypeStruct((m, n), BF16)]
    cast_args = []
    for w, gain in weight_casts:
        r, c = w.shape
        if r % ni == 0 and c % nj == 0:
            blk, idx, gidx = (r // ni, c // nj), (lambda i, j: (i, j)), (lambda i, j: (i, 0))
        else:
            blk, idx, gidx = (r // nj, c // ni), (lambda i, j: (j, i)), (lambda i, j: (j, 0))
        assert blk[0] * (r // blk[0]) == r and blk[1] * (c // blk[1]) == c and (r // blk[0]) * (c // blk[1]) == ni * nj
        in_specs.append(pl.BlockSpec(blk, idx))
        cast_args.append(w)
        if gain is not None:
            in_specs.append(pl.BlockSpec((blk[0], 1), gidx))
            cast_args.append(gain.reshape(r, 1))
        out_specs.append(pl.BlockSpec(blk, idx))
        out_shape.append(jax.ShapeDtypeStruct((r, c), BF16))
    out = pl.pallas_call(
        functools.partial(_ffn_up_kernel, row_splits=4 if tm % 1024 == 0 else 1,
                          cast_has_gain=tuple(g is not None for _, g in weight_casts)),
        grid=(ni, nj),
        in_specs=in_specs,
        out_specs=out_specs,
        out_shape=out_shape,
        scratch_shapes=[pltpu.VMEM((tm, 1), F32)],
        compiler_params=_cparams("parallel", "arbitrary"),
        name="ffn_up",
    )(xb, ssq, wg, wu, *cast_args)
    return out[0], out[1:]


def _gmlp_kernel(u_ref, v_ref, vg_ref, ws_ref, bs_ref, o_ref, vn_ref, *, chunk, groups, gdim):
    v = _gelu_tanh(v_ref[...])
    vn_ref[...] = ((v * _rms_scale(v)) * vg_ref[...]).astype(vn_ref.dtype)
    tm = u_ref.shape[0]
    for c in range(tm // chunk):
        rows = slice(c * chunk, (c + 1) * chunk)
        for g in range(groups):
            cols = slice(g * gdim, (g + 1) * gdim)
            mixed = jnp.dot(ws_ref[g], vn_ref[rows, cols], preferred_element_type=F32) + bs_ref[g]
            o_ref[rows, cols] = (_gelu_tanh(u_ref[rows, cols]) * mixed).astype(o_ref.dtype)


def gmlp_gate(uv, v_norm, ws, bs_b):
    m = uv.shape[0]
    groups, chunk, _ = ws.shape
    gdim = bs_b.shape[2]
    w = groups * gdim
    tm = _tile(m, ROW_PASS_TILE // chunk * chunk)
    return pl.pallas_call(
        functools.partial(_gmlp_kernel, chunk=chunk, groups=groups, gdim=gdim),
        grid=(m // tm,),
        in_specs=[
            pl.BlockSpec((tm, w), lambda i: (i, 0)),
            pl.BlockSpec((tm, w), lambda i: (i, 1)),
            pl.BlockSpec((1, w), lambda i: (0, 0)),
            pl.BlockSpec((groups, chunk, chunk), lambda i: (0, 0, 0)),
            pl.BlockSpec((groups, chunk, gdim), lambda i: (0, 0, 0)),
        ],
        out_specs=pl.BlockSpec((tm, w), lambda i: (i, 0)),
        out_shape=jax.ShapeDtypeStruct((m, w), BF16),
        scratch_shapes=[pltpu.VMEM((tm, w), BF16)],
        compiler_params=_cparams("parallel"),
        name="gmlp_gate",
    )(uv, uv, v_norm.reshape(1, w), ws, bs_b)


def _rope_block(blk, c1, c2):
    return blk * c1 + pltpu.roll(blk, LANES // 2, 1) * c2


def _q_proj_kernel(ql_ref, g_ref, w_ref, c1_ref, c2_ref, o_ref, hn_ref, *, heads_per_tile):
    @pl.when(pl.program_id(1) == 0)
    def _():
        x = ql_ref[...]
        hn_ref[...] = ((x * _rms_scale(x)) * g_ref[...]).astype(hn_ref.dtype)

    acc = jnp.dot(hn_ref[...], w_ref[...], preferred_element_type=F32)
    c1, c2 = c1_ref[...], c2_ref[...]
    for h in range(heads_per_tile):
        lo = h * HEAD_PAD
        o_ref[:, lo:lo + LANES] = (acc[:, lo:lo + LANES] * QK_LOG2_SCALE).astype(o_ref.dtype)
        roped = _rope_block(acc[:, lo + LANES:lo + HEAD_PAD], c1, c2)
        o_ref[:, lo + LANES:lo + HEAD_PAD] = (roped * QK_LOG2_SCALE).astype(o_ref.dtype)


def q_proj(lat, q_norm, w_uq_p, c1, c2, q_rank, seq):
    m = lat.shape[0]
    n = w_uq_p.shape[1]
    tm = _tile(seq, MM_TILE)
    tn = n
    pos_blocks = seq // tm
    return pl.pallas_call(
        functools.partial(_q_proj_kernel, heads_per_tile=tn // HEAD_PAD),
        grid=(m // tm, n // tn),
        in_specs=[
            pl.BlockSpec((tm, q_rank), lambda i, j: (i, 0)),
            pl.BlockSpec((1, q_rank), lambda i, j: (0, 0)),
            pl.BlockSpec((q_rank, tn), lambda i, j: (0, j)),
            pl.BlockSpec((tm, LANES), lambda i, j: (i % pos_blocks, 0)),
            pl.BlockSpec((tm, LANES), lambda i, j: (i % pos_blocks, 0)),
        ],
        out_specs=pl.BlockSpec((tm, tn), lambda i, j: (i, j)),
        out_shape=jax.ShapeDtypeStruct((m, n), BF16),
        scratch_shapes=[pltpu.VMEM((tm, q_rank), BF16)],
        compiler_params=_cparams("parallel", "arbitrary"),
        name="q_proj",
    )(lat, q_norm.reshape(1, q_rank), w_uq_p, c1, c2)


def _kv_proj_kernel(kvl_ref, kr_ref, g_ref, wk_ref, wv_ref, c1_ref, c2_ref, k_ref, v_ref, *, heads):
    x = kvl_ref[...]
    hn = ((x * _rms_scale(x)) * g_ref[...]).astype(BF16)
    kn = jnp.dot(hn, wk_ref[...], preferred_element_type=F32)
    v_ref[...] = jnp.dot(hn, wv_ref[...], preferred_element_type=F32).astype(v_ref.dtype)
    kr = _rope_block(kr_ref[...], c1_ref[...], c2_ref[...]).astype(k_ref.dtype)
    for h in range(heads):
        lo = h * HEAD_PAD
        k_ref[:, lo:lo + LANES] = kn[:, h * QK_NOPE_DIM:(h + 1) * QK_NOPE_DIM].astype(k_ref.dtype)
        k_ref[:, lo + LANES:lo + HEAD_PAD] = kr


def kv_proj(lat, kv_norm, w_k, w_v, c1, c2, q_rank, kv_rank, heads, seq):
    m = lat.shape[0]
    tm = _tile(seq, LATENT_ROW_TILE)
    pos_blocks = seq // tm
    kv_blk = q_rank // kv_rank
    kr_blk = (q_rank + kv_rank) // LANES
    return pl.pallas_call(
        functools.partial(_kv_proj_kernel, heads=heads),
        grid=(m // tm,),
        in_specs=[
            pl.BlockSpec((tm, kv_rank), lambda i: (i, kv_blk)),
            pl.BlockSpec((tm, LANES), lambda i: (i, kr_blk)),
            pl.BlockSpec((1, kv_rank), lambda i: (0, 0)),
            pl.BlockSpec(w_k.shape, lambda i: (0, 0)),
            pl.BlockSpec(w_v.shape, lambda i: (0, 0)),
            pl.BlockSpec((tm, LANES), lambda i: (i % pos_blocks, 0)),
            pl.BlockSpec((tm, LANES), lambda i: (i % pos_blocks, 0)),
        ],
        out_specs=[
            pl.BlockSpec((tm, heads * HEAD_PAD), lambda i: (i, 0)),
            pl.BlockSpec((tm, heads * V_HEAD_DIM), lambda i: (i, 0)),
        ],
        out_shape=[
            jax.ShapeDtypeStruct((m, heads * HEAD_PAD), BF16),
            jax.ShapeDtypeStruct((m, heads * V_HEAD_DIM), BF16),
        ],
        compiler_params=_cparams("parallel"),
        name="kv_proj",
    )(lat, lat, kv_norm.reshape(1, kv_rank), w_k, w_v, c1, c2)


def _attn_kernel(q_ref, k_ref, v_ref, o_ref, *, tq, tk):
    seq = k_ref.shape[0]
    nq, nk = seq // tq, seq // tk
    kv_unroll = min(nk, ATTN_UNROLL_CHUNKS)
    q_unroll = max(1, min(nq, ATTN_UNROLL_CHUNKS // kv_unroll))

    def q_body(qi, _):
        q0 = pl.multiple_of(qi * tq, tq)
        q = q_ref[pl.ds(q0, tq), :]

        def kv_body(ki, carry):
            m, l, acc = carry
            k0 = pl.multiple_of(ki * tk, tk)
            k = k_ref[pl.ds(k0, tk), :]
            s = lax.dot_general(q, k, (((1,), (1,)), ((), ())), preferred_element_type=F32)
            m_new = jnp.maximum(m, jnp.max(s, axis=-1, keepdims=True))
            p = jnp.exp2(s - m_new)
            alpha = jnp.exp2(m - m_new)
            l = alpha * l + jnp.sum(p, axis=-1, keepdims=True)
            pv = jnp.dot(p.astype(BF16), v_ref[pl.ds(k0, tk), :], preferred_element_type=F32)
            return m_new, l, alpha * acc + pv

        init = (jnp.full((tq, 1), -jnp.inf, F32), jnp.zeros((tq, 1), F32), jnp.zeros((tq, V_HEAD_DIM), F32))
        _, l, acc = lax.fori_loop(0, nk, kv_body, init, unroll=kv_unroll)
        o_ref[pl.ds(q0, tq), :] = (acc / l).astype(o_ref.dtype)
        return 0

    lax.fori_loop(0, nq, q_body, 0, unroll=q_unroll)


def attention(q, k, v, batch, seq, heads):
    m = q.shape[0]
    tq, tk = _tile(seq, ATTN_TILE), _tile(seq, ATTN_TILE)
    return pl.pallas_call(
        functools.partial(_attn_kernel, tq=tq, tk=tk),
        grid=(batch, heads),
        in_specs=[
            pl.BlockSpec((seq, HEAD_PAD), lambda b, h: (b, h)),
            pl.BlockSpec((seq, HEAD_PAD), lambda b, h: (b, h)),
            pl.BlockSpec((seq, V_HEAD_DIM), lambda b, h: (b, h)),
        ],
        out_specs=pl.BlockSpec((seq, V_HEAD_DIM), lambda b, h: (b, h)),
        out_shape=jax.ShapeDtypeStruct((m, heads * V_HEAD_DIM), BF16),
        compiler_params=_cparams("parallel", "parallel"),
        name="attention",
    )(q, k, v)


def _merge_kernel(a_ref, b_ref, wa_ref, wb_ref, ga_ref, gb_ref, o_ref):
    ya = jnp.dot(a_ref[...], wa_ref[...], preferred_element_type=F32)
    yb = jnp.dot(b_ref[...], wb_ref[...], preferred_element_type=F32)
    o_ref[...] = (_sigmoid(ga_ref[...]) * ya + _sigmoid(gb_ref[...]) * yb).astype(o_ref.dtype)


def gated_merge(a, b, wa, wb, gates):
    m, ka = a.shape
    kb = b.shape[1]
    n = wa.shape[1]
    tm, tn = _tile(m, MM_TILE), _tile(n, MM_TILE // 2)
    nj = n // tn
    return pl.pallas_call(
        _merge_kernel,
        grid=(m // tm, nj),
        in_specs=[
            pl.BlockSpec((tm, ka), lambda i, j: (i, 0)),
            pl.BlockSpec((tm, kb), lambda i, j: (i, 0)),
            pl.BlockSpec((ka, tn), lambda i, j: (0, j)),
            pl.BlockSpec((kb, tn), lambda i, j: (0, j)),
            pl.BlockSpec((tm, tn), lambda i, j: (i, j)),
            pl.BlockSpec((tm, tn), lambda i, j: (i, j + nj)),
        ],
        out_specs=pl.BlockSpec((tm, tn), lambda i, j: (i, j)),
        out_shape=jax.ShapeDtypeStruct((m, n), BF16),
        compiler_params=_cparams("parallel", "parallel"),
        name="gated_merge",
    )(a, b, wa, wb, gates, gates)


def _ple_gate_kernel(xb_ref, ssq_ref, wg_ref, p_ref, wp_ref, o_ref, rs_ref):
    @pl.when(pl.program_id(1) == 0)
    def _():
        rs_ref[...] = _row_scale(ssq_ref, xb_ref.shape[1])

    g = jnp.dot(xb_ref[...], wg_ref[...], preferred_element_type=F32) * rs_ref[...]
    pp = jnp.dot(p_ref[...].astype(BF16), wp_ref[...], preferred_element_type=F32)
    o_ref[...] = _sigmoid(g) * pp


def ple_gate(xb, ssq, wg, p, wp):
    m, k = xb.shape
    kp = p.shape[1]
    n = wg.shape[1]
    tm, tn = _tile(m, MM_TILE), _tile(n, MM_TILE)
    return pl.pallas_call(
        _ple_gate_kernel,
        grid=(m // tm, n // tn),
        in_specs=[
            pl.BlockSpec((tm, k), lambda i, j: (i, 0)),
            pl.BlockSpec((tm, LANES), lambda i, j: (i, 0)),
            pl.BlockSpec((k, tn), lambda i, j: (0, j)),
            pl.BlockSpec((tm, kp), lambda i, j: (i, 0)),
            pl.BlockSpec((kp, tn), lambda i, j: (0, j)),
        ],
        out_specs=pl.BlockSpec((tm, tn), lambda i, j: (i, j)),
        out_shape=jax.ShapeDtypeStruct((m, n), F32),
        scratch_shapes=[pltpu.VMEM((tm, 1), F32)],
        compiler_params=_cparams("parallel", "arbitrary"),
        name="ple_gate",
    )(xb, ssq, wg, p, wp)


def _final_kernel(x_ref, t_ref, gp_ref, gf_ref, o_ref):
    t = t_ref[...]
    x = x_ref[...] + (t * _rms_scale(t)) * gp_ref[...]
    o_ref[...] = (x * _rms_scale(x)) * gf_ref[...]


def ple_add_final_norm(x, t, post_g, final_g):
    m, d = x.shape
    tm = _tile(m, NORM_PASS_TILE)
    row = pl.BlockSpec((tm, d), lambda i: (i, 0))
    vec = pl.BlockSpec((1, d), lambda i: (0, 0))
    return pl.pallas_call(
        _final_kernel,
        grid=(m // tm,),
        in_specs=[row, row, vec, vec],
        out_specs=row,
        out_shape=jax.ShapeDtypeStruct((m, d), F32),
        compiler_params=_cparams("parallel"),
        name="ple_add_final_norm",
    )(x, t, post_g.reshape(1, d), final_g.reshape(1, d))


def _rope_tables(seq):
    inv = 1.0 / (ROPE_THETA ** (jnp.arange(0, QK_ROPE_DIM, 2, dtype=F32) / QK_ROPE_DIM))
    ang = jnp.arange(seq, dtype=F32)[:, None] * inv[None, :]
    cos, sin = jnp.cos(ang), jnp.sin(ang)
    zero = jnp.zeros((seq, LANES - QK_ROPE_DIM), F32)
    return jnp.concatenate([cos, cos, zero], axis=1), jnp.concatenate([-sin, sin, zero], axis=1)


def _rotate_half_cols(w):
    half = QK_ROPE_DIM // 2
    return jnp.concatenate([w[..., half:], w[..., :half]], axis=-1)


def _prepare_weights(ffn1_norm, ffn1_wg, ffn1_wu, ffn1_wd, mix_norm, w_in, gm_ws, gm_bs, w_out_a, w_uq, w_ukv,
                     w_out_b, w_out, ffn2_norm, ffn2_wg, ffn2_wu, ffn2_wd, ple_gate_norm, w_ple_gate, w_ple_proj,
                     q_rank, kv_rank, gm_width):
    def up(w, gain):
        return (gain[:, None] * w).astype(BF16)

    w_in_t = up(w_in, mix_norm).T

    def in_cols(lo, hi):
        return w_in_t[lo:hi]

    o_q = 2 * gm_width
    o_kv = o_q + q_rank
    o_kr = o_kv + kv_rank
    o_gate = o_kr + QK_ROPE_DIM
    w_kr = in_cols(o_kr, o_gate)
    half = QK_ROPE_DIM // 2
    w_lat = jnp.concatenate([in_cols(o_q, o_kr), w_kr, w_kr[half:], w_kr[:half]], axis=0)

    heads = w_uq.shape[1] // (QK_NOPE_DIM + QK_ROPE_DIM)
    uq = w_uq.reshape(q_rank, heads, QK_NOPE_DIM + QK_ROPE_DIM)
    uq_rope = uq[:, :, QK_NOPE_DIM:]
    w_uq_p = jnp.concatenate([uq[:, :, :QK_NOPE_DIM], uq_rope, _rotate_half_cols(uq_rope)], axis=2)
    ukv = w_ukv.reshape(kv_rank, heads, QK_NOPE_DIM + V_HEAD_DIM)

    gdim = gm_width // gm_ws.shape[0]
    return dict(
        ffn1_up=(up(ffn1_wg, ffn1_norm), up(ffn1_wu, ffn1_norm)),
        late_casts=((ffn1_wd, None), (ffn2_wg, ffn2_norm), (ffn2_wu, ffn2_norm), (ffn2_wd, None)),
        w_uv=in_cols(0, o_q),
        w_lat=w_lat,
        w_gates=in_cols(o_gate, w_in.shape[1]),
        ws=gm_ws.astype(BF16),
        bs_b=jnp.broadcast_to(gm_bs[:, :, None], gm_bs.shape + (gdim,)).astype(F32),
        w_out_a=w_out_a.astype(BF16),
        w_uq_p=w_uq_p.reshape(q_rank, heads * HEAD_PAD).astype(BF16),
        w_k=ukv[:, :, :QK_NOPE_DIM].reshape(kv_rank, heads * QK_NOPE_DIM).astype(BF16),
        w_v=ukv[:, :, QK_NOPE_DIM:].reshape(kv_rank, heads * V_HEAD_DIM).astype(BF16),
        w_out_b=w_out_b.astype(BF16),
        w_out=w_out.astype(BF16),
        w_ple_gate=up(w_ple_gate, ple_gate_norm),
        w_ple_proj=w_ple_proj.astype(BF16),
        heads=heads,
    )


def _trunk(x3d, p3d, w, norms, q_rank, kv_rank, late_weights):
    batch, seq, d = x3d.shape
    x = x3d.reshape(batch * seq, d)
    p = p3d.reshape(batch * seq, p3d.shape[-1])
    heads = w["heads"]

    xb, ssq = cast_sumsq(x)
    a, casted = ffn_up(xb, ssq, *w["ffn1_up"], weight_casts=() if late_weights else w["late_casts"])
    ffn1_wd, ffn2_wg, ffn2_wu, ffn2_wd = late_weights = late_weights or casted
    x, xb, ssq = matmul_residual(a, ffn1_wd, x, 0.5, *FFN_DOWN_TILE, "ffn_down", emit_norm_inputs=True)

    uv = matmul_normed(xb, ssq, w["w_uv"], F32, MM_TILE, MM_TILE, "in_proj_uv")
    lat = matmul_normed(xb, ssq, w["w_lat"], F32, LATENT_ROW_TILE, w["w_lat"].shape[0], "in_proj_lat")
    gates = matmul_normed(xb, ssq, w["w_gates"], F32, MM_TILE, MM_TILE, "in_proj_gates")

    gm = gmlp_gate(uv, norms["gm_v"], w["ws"], w["bs_b"])

    c1, c2 = _rope_tables(seq)
    q = q_proj(lat, norms["q"], w["w_uq_p"], c1, c2, q_rank, seq)
    k, v = kv_proj(lat, norms["kv"], w["w_k"], w["w_v"], c1, c2, q_rank, kv_rank, heads, seq)
    o = attention(q, k, v, batch, seq, heads)

    merged = gated_merge(gm, o, w["w_out_a"], w["w_out_b"], gates)
    x, xb, ssq = matmul_residual(merged, w["w_out"], x, 1.0, MM_TILE, MM_TILE // 2, "out_proj", emit_norm_inputs=True)

    a, _ = ffn_up(xb, ssq, ffn2_wg, ffn2_wu)
    x, xb, ssq = matmul_residual(a, ffn2_wd, x, 0.5, *FFN_DOWN_TILE, "ffn_down", emit_norm_inputs=True)

    t = ple_gate(xb, ssq, w["w_ple_gate"], p, w["w_ple_proj"])
    y = ple_add_final_norm(x, t, norms["ple_post"], norms["final"])
    return y.reshape(batch, seq, d), late_weights


def kernel(x_prompt, x_sample, p_prompt, p_sample, ffn1_norm, ffn1_wg, ffn1_wu, ffn1_wd, mix_norm, w_in, gm_v_norm, gm_ws, gm_bs, w_out_a, q_norm, w_uq, kv_norm, w_ukv, w_out_b, w_out, ffn2_norm, ffn2_wg, ffn2_wu, ffn2_wd, ple_gate_norm, w_ple_gate, w_ple_proj, ple_post_norm, final_norm):
    depth = ffn1_norm.shape[0]
    q_rank = q_norm.shape[1]
    kv_rank = kv_norm.shape[1]
    gm_width = gm_v_norm.shape[1]
    assert q_rank % kv_rank == 0 and (q_rank + kv_rank) % LANES == 0

    xs = [x_prompt, x_sample]
    for l in range(depth):
        w = _prepare_weights(ffn1_norm[l], ffn1_wg[l], ffn1_wu[l], ffn1_wd[l], mix_norm[l], w_in[l], gm_ws[l],
                             gm_bs[l], w_out_a[l], w_uq[l], w_ukv[l], w_out_b[l], w_out[l], ffn2_norm[l], ffn2_wg[l],
                             ffn2_wu[l], ffn2_wd[l], ple_gate_norm[l], w_ple_gate[l], w_ple_proj[l], q_rank, kv_rank,
                             gm_width)
        norms = dict(gm_v=gm_v_norm[l], q=q_norm[l], kv=kv_norm[l], ple_post=ple_post_norm[l], final=final_norm)
        assert depth == 1
        late_weights = None
        for g, p in enumerate([p_prompt, p_sample]):
            xs[g], late_weights = _trunk(xs[g], p[l], w, norms, q_rank, kv_rank, late_weights)
    return tuple(xs)
```
